```python
import jax, jax.numpy as jnp
from jax import lax
import numpy as np

D_MODEL = 1024
BATCH = 8
SEQ = 4096
DEPTH = 4

N_META = 16
FOX_HEAD_DIM = 128
FOX_WIDTH = D_MODEL
FOX_HEADS = FOX_WIDTH // FOX_HEAD_DIM
FOX_BLOCK = 128
MLSTM_WIDTH = 2 * D_MODEL
MLSTM_HEADS = 4
MLSTM_V_DIM = MLSTM_WIDTH // MLSTM_HEADS
MLSTM_QK_DIM = MLSTM_V_DIM // 2
MLSTM_CHUNK = 128
CONV_WIDTH = 4
EPS = 1e-6

SPLIT_SIZES = (
    FOX_WIDTH, FOX_WIDTH, FOX_WIDTH,
    FOX_HEADS,
    FOX_WIDTH,
    2 * MLSTM_HEADS * MLSTM_QK_DIM,
    MLSTM_WIDTH,
    MLSTM_HEADS, MLSTM_HEADS,
    MLSTM_WIDTH,
    MLSTM_WIDTH,
    D_MODEL, D_MODEL,
)
N_IN = sum(SPLIT_SIZES)
SPLIT_POINTS = tuple(int(p) for p in np.cumsum(SPLIT_SIZES)[:-1])

kernel_name = 'hybrid_fox_mlstm_block'


def rms_norm(x, gain):
    xf = x.astype(jnp.float32)
    y = xf * lax.rsqrt(jnp.mean(xf * xf, axis=-1, keepdims=True) + EPS)
    return (y * gain.astype(jnp.float32)).astype(x.dtype)


def causal_conv(x, w, b):
    y = lax.conv_general_dilated(x, w[:, None, :], window_strides=(1,),
                                 padding=[(CONV_WIDTH - 1, 0)],
                                 dimension_numbers=('NWC', 'WIO', 'NWC'),
                                 feature_group_count=x.shape[-1])
    return y + b


def forgetting_attention(q, k, v, log_f):
    L = q.shape[1]
    cum = jnp.cumsum(log_f, axis=1).transpose(0, 2, 1)
    scale = FOX_HEAD_DIM ** -0.5
    edges = [0] + list(range(N_META, L + 1, FOX_BLOCK))
    outs = []
    for s0, s1 in zip(edges[:-1], edges[1:]):
        scores = jnp.einsum('bqhd,bkhd->bhqk', q[:, s0:s1], k[:, :s1]).astype(jnp.float32) * scale
        decay = cum[:, :, s0:s1, None] - cum[:, :, None, :s1]
        causal = jnp.arange(s0, s1)[:, None] >= jnp.arange(s1)[None, :]
        p = jax.nn.softmax(jnp.where(causal, scores + decay, -jnp.inf), axis=-1)
        outs.append(jnp.einsum('bhqk,bkhd->bqhd', p.astype(v.dtype), v[:, :s1]))
    return jnp.concatenate(outs, axis=1)


def mlstm_chunk(state, chunk):
    c_mat, n_vec, m = state
    q, k, v, log_i, log_f = chunk
    T = q.shape[2]
    b = jnp.cumsum(log_f, axis=-1)
    causal = jnp.tril(jnp.ones((T, T), dtype=bool))
    log_d = jnp.where(causal, b[..., :, None] - b[..., None, :] + log_i[..., None, :], -jnp.inf)
    log_inter = b + m[..., None]
    m_t = jnp.maximum(log_inter, jnp.max(log_d, axis=-1))
    d = jnp.exp(log_d - m_t[..., None])
    inter = jnp.exp(log_inter - m_t)
    s = jnp.einsum('bhtd,bhsd->bhts', q, k) * d
    num = inter[..., None] * jnp.einsum('bhtd,bhde->bhte', q, c_mat) + jnp.einsum('bhts,bhse->bhte', s, v)
    den = inter * jnp.einsum('bhtd,bhd->bht', q, n_vec) + jnp.sum(s, axis=-1)
    h = num / jnp.maximum(jnp.abs(den), jnp.exp(-m_t))[..., None]
    b_last = b[..., -1]
    log_w = b_last[..., None] - b + log_i
    m_new = jnp.maximum(b_last + m, jnp.max(log_w, axis=-1))
    w = jnp.exp(log_w - m_new[..., None])
    decay = jnp.exp(b_last + m - m_new)
    kw = k * w[..., None]
    c_new = decay[..., None, None] * c_mat + jnp.einsum('bhsd,bhse->bhde', kw, v)
    n_new = decay[..., None] * n_vec + jnp.sum(kw, axis=2)
    return (c_new, n_new, m_new), h


def mlstm_scan(q, k, v, log_i, log_f):
    B, L, H, _ = q.shape
    n_real = L - N_META
    nc = n_real // MLSTM_CHUNK
    f32 = jnp.float32
    q, k, v = (t.astype(f32).transpose(0, 2, 1, 3) for t in (q, k, v))
    log_i, log_f = log_i.transpose(0, 2, 1), log_f.transpose(0, 2, 1)
    state = (jnp.zeros((B, H, MLSTM_QK_DIM, MLSTM_V_DIM), f32),
             jnp.zeros((B, H, MLSTM_QK_DIM), f32),
             jnp.zeros((B, H), f32))
    state, h_meta = mlstm_chunk(state, tuple(t[:, :, :N_META] for t in (q, k, v, log_i, log_f)))

    def to_chunks(t):
        t = t[:, :, N_META:]
        t = t.reshape((B, H, nc, MLSTM_CHUNK) + t.shape[3:])
        return jnp.moveaxis(t, 2, 0)

    _, h_real = lax.scan(mlstm_chunk, state, tuple(to_chunks(t) for t in (q, k, v, log_i, log_f)))
    h_real = jnp.moveaxis(h_real, 0, 2).reshape(B, H, n_real, MLSTM_V_DIM)
    return jnp.concatenate([h_meta, h_real], axis=2).transpose(0, 2, 1, 3)


def hybrid_layer(x, g_pre, g_post, w_in, b_fox_f, conv_w, conv_b, b_i, b_f, g_head, w_a, w_b, w_o):
    B, L, _ = x.shape
    h = rms_norm(x, g_pre)
    z = h @ w_in
    (fq, fk, fv, ff, fz, mqk, mv, mi, mf, mo, mz, ga, gb) = jnp.split(z, SPLIT_POINTS, axis=-1)

    log_fa = jax.nn.log_sigmoid((ff + b_fox_f).astype(jnp.float32))
    oa = forgetting_attention(fq.reshape(B, L, FOX_HEADS, FOX_HEAD_DIM),
                              fk.reshape(B, L, FOX_HEADS, FOX_HEAD_DIM),
                              fv.reshape(B, L, FOX_HEADS, FOX_HEAD_DIM), log_fa).reshape(B, L, FOX_WIDTH)
    ya = (oa * jax.nn.silu(fz)) @ w_a

    qk = jax.nn.silu(causal_conv(mqk, conv_w, conv_b))
    mq, mk = jnp.split(qk, 2, axis=-1)
    mq = mq.reshape(B, L, MLSTM_HEADS, MLSTM_QK_DIM) * (MLSTM_QK_DIM ** -0.5)
    mk = mk.reshape(B, L, MLSTM_HEADS, MLSTM_QK_DIM)
    log_ib = (mi + b_i).astype(jnp.float32)
    log_fb = jax.nn.log_sigmoid((mf + b_f).astype(jnp.float32))
    hb = mlstm_scan(mq, mk, mv.reshape(B, L, MLSTM_HEADS, MLSTM_V_DIM), log_ib, log_fb).astype(x.dtype)
    hb = jax.nn.sigmoid(mo).reshape(B, L, MLSTM_HEADS, MLSTM_V_DIM) * hb
    hb = rms_norm(hb, g_head.reshape(MLSTM_HEADS, MLSTM_V_DIM)).reshape(B, L, MLSTM_WIDTH)
    yb = (hb * jax.nn.silu(mz)) @ w_b

    merged = jax.nn.sigmoid(ga) * ya + jax.nn.sigmoid(gb) * yb
    return x + rms_norm(merged @ w_o, g_post)


def setup_inputs(seed: int = 0) -> dict:
    key = jax.random.key(seed)
    ks = jax.random.split(key, 14)

    def normal(k, shape, scale):
        return jax.random.normal(k, shape, jnp.float32) * scale

    qk_cols = 2 * MLSTM_HEADS * MLSTM_QK_DIM
    return {
        'x': normal(ks[0], (BATCH, SEQ, D_MODEL), 1.0),
        'meta_tokens': normal(ks[1], (N_META, D_MODEL), 1.0),
        'norm_pre': 1.0 + normal(ks[2], (DEPTH, D_MODEL), 0.02),
        'norm_post': 1.0 + normal(ks[3], (DEPTH, D_MODEL), 0.02),
        'w_in': normal(ks[4], (DEPTH, D_MODEL, N_IN), D_MODEL ** -0.5),
        'b_fox_f': jnp.linspace(1.0, 5.0, FOX_HEADS, dtype=jnp.float32)[None] + normal(ks[5], (DEPTH, FOX_HEADS), 0.1),
        'conv_w': normal(ks[6], (DEPTH, CONV_WIDTH, qk_cols), CONV_WIDTH ** -0.5),
        'conv_b': normal(ks[7], (DEPTH, qk_cols), 0.02),
        'b_mlstm_i': normal(ks[8], (DEPTH, MLSTM_HEADS), 0.1),
        'b_mlstm_f': jnp.linspace(3.0, 6.0, MLSTM_HEADS, dtype=jnp.float32)[None] + normal(ks[9], (DEPTH, MLSTM_HEADS), 0.1),
        'mlstm_head_norm': 1.0 + normal(ks[10], (DEPTH, MLSTM_WIDTH), 0.02),
        'w_a': normal(ks[11], (DEPTH, FOX_WIDTH, D_MODEL), FOX_WIDTH ** -0.5),
        'w_b': normal(ks[12], (DEPTH, MLSTM_WIDTH, D_MODEL), MLSTM_WIDTH ** -0.5),
        'w_o': normal(ks[13], (DEPTH, D_MODEL, D_MODEL), D_MODEL ** -0.5),
    }


def reference(x, meta_tokens, norm_pre, norm_post, w_in, b_fox_f, conv_w, conv_b,
              b_mlstm_i, b_mlstm_f, mlstm_head_norm, w_a, w_b, w_o):
    B = x.shape[0]
    meta = jnp.broadcast_to(meta_tokens[None].astype(x.dtype), (B, N_META, x.shape[-1]))
    h = jnp.concatenate([meta, x], axis=1)
    for l in range(DEPTH):
        h = hybrid_layer(h, norm_pre[l], norm_post[l], w_in[l], b_fox_f[l], conv_w[l], conv_b[l],
                         b_mlstm_i[l], b_mlstm_f[l], mlstm_head_norm[l], w_a[l], w_b[l], w_o[l])
    return h[:, N_META:]
```

```python
import functools

import jax
import jax.numpy as jnp
import numpy as np
from jax import lax
from jax.experimental import pallas as pl
from jax.experimental.pallas import tpu as pltpu

F32 = jnp.float32
BF16 = jnp.bfloat16

D_MODEL = 1024
N_META = 16
PAD = 128
N_ZERO = PAD - N_META
FOX_HEADS = 8
FOX_DH = 128
FOX_W = FOX_HEADS * FOX_DH
ML_HEADS = 4
ML_DK = 256
ML_DV = 512
ML_W = ML_HEADS * ML_DV
CHUNK = 128
CONV_W = 4
EPS = 1e-6
NEG = -1e30

SPLIT_SIZES = (FOX_W, FOX_W, FOX_W, FOX_HEADS, FOX_W, 2 * ML_HEADS * ML_DK, ML_W,
               ML_HEADS, ML_HEADS, ML_W, ML_W, D_MODEL, D_MODEL)
SPLIT_POINTS = tuple(int(p) for p in np.cumsum(SPLIT_SIZES)[:-1])

Z_FQ, Z_FK, Z_FV, Z_FZ = 0, 1024, 2048, 3072
Z_MQ, Z_MK, Z_MV, Z_MO, Z_MZ = 4096, 5120, 6144, 8192, 10240
Z_GA, Z_GB = 12288, 13312
NZ = 14336
G_FOX, G_LI, G_LF = 0, 8, 12

VMEM_LIMIT = 48 * 1024 * 1024


def _largest_divisor(n, candidates):
    for c in candidates:
        if n % c == 0:
            return c
    raise ValueError(f"no tile in {candidates} divides {n}")


def _inproj_kernel(x_ref, g_ref, w_ref, wg_ref, z_ref, zg_ref, xn_ref):
    @pl.when(pl.program_id(1) == 0)
    def _():
        x = x_ref[...]
        ms = jnp.mean(x * x, axis=-1, keepdims=True)
        xn = (x * lax.rsqrt(ms + EPS) * g_ref[...]).astype(BF16)
        xn_ref[...] = xn
        zg_ref[...] = jnp.dot(xn, wg_ref[...], preferred_element_type=F32)

    z_ref[...] = jnp.dot(xn_ref[...], w_ref[...],
                         preferred_element_type=F32).astype(z_ref.dtype)


def _inproj(x2, g, wz, wg):
    t = x2.shape[0]
    tm = _largest_divisor(t, (1024, 512, 256, 128))
    tn = 512
    return pl.pallas_call(
        _inproj_kernel,
        out_shape=(jax.ShapeDtypeStruct((t, NZ), BF16),
                   jax.ShapeDtypeStruct((t, 128), F32)),
        grid=(t // tm, NZ // tn),
        in_specs=[pl.BlockSpec((tm, D_MODEL), lambda i, j: (i, 0)),
                  pl.BlockSpec((1, D_MODEL), lambda i, j: (0, 0)),
                  pl.BlockSpec((D_MODEL, tn), lambda i, j: (0, j)),
                  pl.BlockSpec((D_MODEL, 128), lambda i, j: (0, 0))],
        out_specs=(pl.BlockSpec((tm, tn), lambda i, j: (i, j)),
                   pl.BlockSpec((tm, 128), lambda i, j: (i, 0))),
        scratch_shapes=[pltpu.VMEM((tm, D_MODEL), BF16)],
        compiler_params=pltpu.CompilerParams(
            dimension_semantics=("parallel", "arbitrary"),
            vmem_limit_bytes=VMEM_LIMIT),
        name="inproj",
    )(x2, g, wz, wg)


def _gates_kernel(zg_ref, bias_ref, gcol_ref, grow_ref, frow_ref, carry_ref):
    c = pl.program_id(1)

    @pl.when(c == 0)
    def _():
        carry_ref[...] = jnp.zeros_like(carry_ref)

    a = zg_ref[0] + bias_ref[...]
    log_sig = jnp.minimum(a, 0.0) - jnp.log1p(jnp.exp(-jnp.abs(a)))
    row = lax.broadcasted_iota(jnp.int32, (CHUNK, 128), 0)
    lane = lax.broadcasted_iota(jnp.int32, (CHUNK, 128), 1)
    is_fox = lane < G_LI
    is_li = (lane >= G_LI) & (lane < G_LF)
    is_lf = (lane >= G_LF) & (lane < G_LF + ML_HEADS)
    real = (c > 0) | (row >= N_ZERO)
    v = jnp.where((is_fox | is_lf) & real, log_sig, 0.0)

    tri = (row >= lane).astype(BF16)
    hi = v.astype(BF16)
    r1 = v - hi.astype(F32)
    mid = r1.astype(BF16)
    lo = (r1 - mid.astype(F32)).astype(BF16)
    cum = (jnp.dot(tri, hi, preferred_element_type=F32)
           + jnp.dot(tri, mid, preferred_element_type=F32)
           + jnp.dot(tri, lo, preferred_element_type=F32))
    cum = cum + carry_ref[...]
    carry_ref[...] = jnp.where(lane[:1] < G_LI, cum[CHUNK - 1:CHUNK, :], 0.0)

    log_i = jnp.where(real, a, NEG)
    out = jnp.where(is_li, log_i, cum)
    gcol_ref[0] = out
    out_t = out.T
    grow_ref[0] = out_t[:16, :]
    for hd in range(FOX_HEADS):
        frow_ref[0, hd] = out_t[G_FOX + hd:G_FOX + hd + 1, :]


def _gates(zg3, bias):
    b, lp, _ = zg3.shape
    return pl.pallas_call(
        _gates_kernel,
        out_shape=(jax.ShapeDtypeStruct((b, lp, 128), F32),
                   jax.ShapeDtypeStruct((b, 16, lp), F32),
                   jax.ShapeDtypeStruct((b, FOX_HEADS, 1, lp), F32)),
        grid=(b, lp // CHUNK),
        in_specs=[pl.BlockSpec((1, CHUNK, 128), lambda i, c: (i, c, 0)),
                  pl.BlockSpec((1, 128), lambda i, c: (0, 0))],
        out_specs=(pl.BlockSpec((1, CHUNK, 128), lambda i, c: (i, c, 0)),
                   pl.BlockSpec((1, 16, CHUNK), lambda i, c: (i, 0, c)),
                   pl.BlockSpec((1, FOX_HEADS, 1, CHUNK), lambda i, c: (i, 0, 0, c))),
        scratch_shapes=[pltpu.VMEM((1, 128), F32)],
        compiler_params=pltpu.CompilerParams(
            dimension_semantics=("parallel", "arbitrary")),
        name="gates",
    )(zg3, bias)


def _fox_kernel(q_ref, k_ref, v_ref, fz_ref, frow_ref, o_ref, *, tq, nq):
    scale = FOX_DH ** -0.5

    def step(q, carry, ks, tk, mask_fn):
        m, l, acc = carry
        k = k_ref[0, pl.ds(ks, tk), :]
        v = v_ref[0, pl.ds(ks, tk), :]
        f = frow_ref[0, 0, :, pl.ds(ks, tk)]
        s = lax.dot_general(q, k, (((1,), (1,)), ((), ())),
                            preferred_element_type=F32) * scale - f
        if mask_fn is not None:
            rows = lax.broadcasted_iota(jnp.int32, s.shape, 0)
            cols = lax.broadcasted_iota(jnp.int32, s.shape, 1)
            s = jnp.where(mask_fn(rows, cols), s, NEG)
        m_new = jnp.maximum(m, jnp.max(s, axis=-1, keepdims=True))
        alpha = jnp.exp(m - m_new)
        p = jnp.exp(s - m_new)
        l = alpha * l + jnp.sum(p, axis=-1, keepdims=True)
        acc = alpha * acc + jnp.dot(p.astype(BF16), v, preferred_element_type=F32)
        return m_new, l, acc

    def init(rows):
        return (jnp.full((rows, 1), NEG, F32), jnp.zeros((rows, 1), F32),
                jnp.zeros((rows, FOX_DH), F32))

    def finish(carry, q0, rows):
        _, l, acc = carry
        fz = fz_ref[0, pl.ds(q0, rows), :].astype(F32)
        o = acc / l * (fz * jax.nn.sigmoid(fz))
        o_ref[0, pl.ds(q0, rows), :] = o.astype(o_ref.dtype)

    q = q_ref[0, pl.ds(0, PAD), :]
    carry = step(q, init(PAD), 0, PAD, lambda r, c: (c >= N_ZERO) & (c <= r))
    finish(carry, 0, PAD)

    def q_block(i, _):
        q0 = pl.multiple_of(PAD + i * tq, 128)
        q = q_ref[0, pl.ds(q0, tq), :]
        carry = step(q, init(tq), 0, PAD, lambda r, c: c >= N_ZERO)

        def kv_block(j, carry):
            ks = pl.multiple_of(PAD + j * tq, 128)
            return step(q, carry, ks, tq, None)

        carry = lax.fori_loop(0, i, kv_block, carry)
        carry = step(q, carry, q0, tq, lambda r, c: c <= r)
        finish(carry, q0, tq)
        return 0

    lax.fori_loop(0, nq, q_block, 0)


def _fox(z3, frow):
    b, lp, _ = z3.shape
    n_real = lp - PAD
    tq = _largest_divisor(n_real, (512, 256, 128))
    blk = lambda off: pl.BlockSpec((1, lp, FOX_DH),
                                   lambda i, h, off=off: (i, 0, off // FOX_DH + h))
    return pl.pallas_call(
        functools.partial(_fox_kernel, tq=tq, nq=n_real // tq),
        out_shape=jax.ShapeDtypeStruct((b, lp, FOX_W), BF16),
        grid=(b, FOX_HEADS),
        in_specs=[blk(Z_FQ), blk(Z_FK), blk(Z_FV), blk(Z_FZ),
                  pl.BlockSpec((1, 1, 1, lp), lambda i, h: (i, h, 0, 0))],
        out_specs=pl.BlockSpec((1, lp, FOX_DH), lambda i, h: (i, 0, h)),
        compiler_params=pltpu.CompilerParams(
            dimension_semantics=("parallel", "arbitrary"),
            vmem_limit_bytes=VMEM_LIMIT),
        name="fox",
    )(z3, z3, z3, z3, frow)


def _mlstm_kernel(mq_ref, mk_ref, mv_ref, mo_ref, mz_ref, gcol_ref, grow_ref,
                  cw_ref, cb_ref, gh_ref, o_ref,
                  xbuf_ref, c_ref, n_ref, m_ref):
    c = pl.program_id(1)

    @pl.when(c == 0)
    def _():
        xbuf_ref[0:8, :] = jnp.zeros((8, 2 * ML_HEADS * ML_DK), F32)
        c_ref[...] = jnp.zeros_like(c_ref)
        n_ref[...] = jnp.zeros_like(n_ref)
        m_ref[...] = jnp.zeros_like(m_ref)

    half = ML_HEADS * ML_DK
    xbuf_ref[8:8 + CHUNK, 0:half] = mq_ref[0].astype(F32)
    xbuf_ref[8:8 + CHUNK, half:2 * half] = mk_ref[0].astype(F32)
    y = cb_ref[...]
    for j in range(CONV_W):
        y = y + cw_ref[j:j + 1, :] * xbuf_ref[pl.ds(8 - (CONV_W - 1) + j, CHUNK), :]
    xbuf_ref[0:8, :] = xbuf_ref[CHUNK:CHUNK + 8, :]
    qk = y * jax.nn.sigmoid(y)

    gcol = gcol_ref[0]
    grow = grow_ref[0]
    rows = lax.broadcasted_iota(jnp.int32, (CHUNK, CHUNK), 0)
    cols = lax.broadcasted_iota(jnp.int32, (CHUNK, CHUNK), 1)
    causal = cols <= rows

    for hd in range(ML_HEADS):
        q = (qk[:, hd * ML_DK:(hd + 1) * ML_DK] * (ML_DK ** -0.5)).astype(BF16)
        k = qk[:, half + hd * ML_DK: half + (hd + 1) * ML_DK]
        v = mv_ref[0, :, hd * ML_DV:(hd + 1) * ML_DV]
        b_col = gcol[:, G_LF + hd:G_LF + hd + 1]
        li_col = gcol[:, G_LI + hd:G_LI + hd + 1]
        b_row = grow[G_LF + hd:G_LF + hd + 1, :]
        li_row = grow[G_LI + hd:G_LI + hd + 1, :]
        m_prev = m_ref[hd:hd + 1, 0:1]
        c_prev = c_ref[hd]
        n_prev = n_ref[hd:hd + 1, :]

        log_d = jnp.where(causal, b_col - b_row + li_row, NEG)
        log_inter = b_col + m_prev
        m_t = jnp.maximum(log_inter, jnp.max(log_d, axis=-1, keepdims=True))
        d = jnp.exp(log_d - m_t)
        inter = jnp.exp(log_inter - m_t)
        s = lax.dot_general(q, k.astype(BF16), (((1,), (1,)), ((), ())),
                            preferred_element_type=F32) * d
        num = (inter * jnp.dot(q, c_prev.astype(BF16), preferred_element_type=F32)
               + jnp.dot(s.astype(BF16), v, preferred_element_type=F32))
        den = (inter * jnp.sum(q.astype(F32) * n_prev, axis=-1, keepdims=True)
               + jnp.sum(s, axis=-1, keepdims=True))
        hh = num / jnp.maximum(jnp.abs(den), jnp.exp(-m_t))

        b_last = b_col[CHUNK - 1:CHUNK, :]
        log_w = b_last - b_col + li_col
        m_new = jnp.maximum(b_last + m_prev, jnp.max(log_w, axis=0, keepdims=True))
        w = jnp.exp(log_w - m_new)
        decay = jnp.exp(b_last + m_prev - m_new)
        kw = k * w
        c_ref[hd] = decay * c_prev + lax.dot_general(
            kw.astype(BF16), v, (((0,), (0,)), ((), ())), preferred_element_type=F32)
        n_ref[hd:hd + 1, :] = decay * n_prev + jnp.sum(kw, axis=0, keepdims=True)
        m_ref[hd:hd + 1, :] = jnp.broadcast_to(m_new, (1, 128))

        mo = mo_ref[0, :, hd * ML_DV:(hd + 1) * ML_DV].astype(F32)
        mz = mz_ref[0, :, hd * ML_DV:(hd + 1) * ML_DV].astype(F32)
        hb = jax.nn.sigmoid(mo) * hh
        ms = jnp.mean(hb * hb, axis=-1, keepdims=True)
        hb = hb * lax.rsqrt(ms + EPS) * gh_ref[:, hd * ML_DV:(hd + 1) * ML_DV]
        o_ref[0, :, hd * ML_DV:(hd + 1) * ML_DV] = (
            hb * (mz * jax.nn.sigmoid(mz))).astype(o_ref.dtype)


def _mlstm(z3, gcol, grow, conv_w, conv_b, g_head):
    b, lp, _ = z3.shape
    half = ML_HEADS * ML_DK
    zblk = lambda width, off: pl.BlockSpec(
        (1, CHUNK, width), lambda i, c, off=off, width=width: (i, c, off // width))
    full = lambda shape: pl.BlockSpec(shape, lambda i, c: (0,) * len(shape))
    return pl.pallas_call(
        _mlstm_kernel,
        out_shape=jax.ShapeDtypeStruct((b, lp, ML_W), BF16),
        grid=(b, lp // CHUNK),
        in_specs=[zblk(half, Z_MQ), zblk(half, Z_MK), zblk(ML_W, Z_MV),
                  zblk(ML_W, Z_MO), zblk(ML_W, Z_MZ),
                  pl.BlockSpec((1, CHUNK, 128), lambda i, c: (i, c, 0)),
                  pl.BlockSpec((1, 16, CHUNK), lambda i, c: (i, 0, c)),
                  full((CONV_W, 2 * half)), full((1, 2 * half)), full((1, ML_W))],
        out_specs=pl.BlockSpec((1, CHUNK, ML_W), lambda i, c: (i, c, 0)),
        scratch_shapes=[pltpu.VMEM((8 + CHUNK, 2 * half), F32),
                        pltpu.VMEM((ML_HEADS, ML_DK, ML_DV), F32),
                        pltpu.VMEM((ML_HEADS, ML_DK), F32),
                        pltpu.VMEM((ML_HEADS, 128), F32)],
        compiler_params=pltpu.CompilerParams(
            dimension_semantics=("parallel", "arbitrary"),
            vmem_limit_bytes=VMEM_LIMIT),
        name="mlstm",
    )(z3, z3, z3, z3, z3, gcol, grow, conv_w, conv_b, g_head)


def _outproj_kernel(oa_ref, hb_ref, ga_ref, gb_ref, x_ref, wa_ref, wb_ref, wo_ref,
                    g_ref, o_ref):
    ya = jnp.dot(oa_ref[...], wa_ref[...], preferred_element_type=F32)
    yb = jnp.dot(hb_ref[...], wb_ref[...], preferred_element_type=F32)
    merged = (jax.nn.sigmoid(ga_ref[...].astype(F32)) * ya
              + jax.nn.sigmoid(gb_ref[...].astype(F32)) * yb)
    y = jnp.dot(merged.astype(BF16), wo_ref[...], preferred_element_type=F32)
    ms = jnp.mean(y * y, axis=-1, keepdims=True)
    o_ref[...] = x_ref[...] + y * lax.rsqrt(ms + EPS) * g_ref[...]


def _outproj(oa2, hb2, z2, x2, wa, wb, wo, g):
    t = x2.shape[0]
    tm = _largest_divisor(t, (512, 256, 128))
    rows = lambda width, col=0: pl.BlockSpec((tm, width), lambda i, col=col: (i, col))
    full = lambda shape: pl.BlockSpec(shape, lambda i: (0,) * len(shape))
    return pl.pallas_call(
        _outproj_kernel,
        out_shape=jax.ShapeDtypeStruct((t, D_MODEL), F32),
        grid=(t // tm,),
        in_specs=[rows(FOX_W), rows(ML_W),
                  rows(D_MODEL, Z_GA // D_MODEL), rows(D_MODEL, Z_GB // D_MODEL),
                  rows(D_MODEL),
                  full((FOX_W, D_MODEL)), full((ML_W, D_MODEL)), full((D_MODEL, D_MODEL)),
                  full((1, D_MODEL))],
        out_specs=rows(D_MODEL),
        compiler_params=pltpu.CompilerParams(
            dimension_semantics=("parallel",),
            vmem_limit_bytes=VMEM_LIMIT),
        name="outproj",
    )(oa2, hb2, z2, z2, x2, wa, wb, wo, g)


def _layer(h, g_pre, g_post, w_in, b_fox_f, conv_w, conv_b, b_i, b_f, g_head, w_a, w_b, w_o):
    b, lp, d = h.shape
    (fq, fk, fv, ff, fz, mqk, mv, mi, mf, mo, mz, ga, gb) = jnp.split(w_in, SPLIT_POINTS, axis=1)
    wz = jnp.concatenate([fq, fk, fv, fz, mqk, mv, mo, mz, ga, gb], axis=1).astype(BF16)
    wg = jnp.concatenate([ff, mi, mf, jnp.zeros((d, 128 - 16), F32)], axis=1).astype(BF16)
    bias = jnp.concatenate([b_fox_f, b_i, b_f, jnp.zeros((128 - 16,), F32)])[None, :]

    x2 = h.reshape(b * lp, d)
    z2, zg2 = _inproj(x2, g_pre[None, :], wz, wg)
    z3 = z2.reshape(b, lp, NZ)
    gcol, grow, frow = _gates(zg2.reshape(b, lp, 128), bias)
    oa = _fox(z3, frow)
    hb = _mlstm(z3, gcol, grow, conv_w, conv_b[None, :], g_head[None, :])
    out = _outproj(oa.reshape(b * lp, FOX_W), hb.reshape(b * lp, ML_W), z2, x2,
                   w_a.astype(BF16), w_b.astype(BF16), w_o.astype(BF16), g_post[None, :])
    return out.reshape(b, lp, d)


def kernel(x, meta_tokens, norm_pre, norm_post, w_in, b_fox_f, conv_w, conv_b,
           b_mlstm_i, b_mlstm_f, mlstm_head_norm, w_a, w_b, w_o):
    b, seq, d = x.shape
    assert d == D_MODEL and seq % CHUNK == 0
    assert meta_tokens.shape == (N_META, D_MODEL)
    lead = jnp.concatenate([jnp.zeros((N_ZERO, d), x.dtype), meta_tokens.astype(x.dtype)], axis=0)
    h = jnp.concatenate([jnp.broadcast_to(lead[None], (b, PAD, d)), x], axis=1)
    for l in range(norm_pre.shape[0]):
        h = _layer(h, norm_pre[l], norm_post[l], w_in[l], b_fox_f[l], conv_w[l], conv_b[l],
                   b_mlstm_i[l], b_mlstm_f[l], mlstm_head_norm[l], w_a[l], w_b[l], w_o[l])
    return h[:, PAD:]
```

```python
import functools

import jax
import jax.numpy as jnp
import numpy as np
from jax import lax
from jax.experimental import pallas as pl
from jax.experimental.pallas import tpu as pltpu

F32 = jnp.float32
BF16 = jnp.bfloat16

D_MODEL = 1024
N_META = 16
PAD = 128
N_ZERO = PAD - N_META
FOX_HEADS = 8
FOX_DH = 128
FOX_W = FOX_HEADS * FOX_DH
ML_HEADS = 4
ML_DK = 256
ML_DV = 512
ML_W = ML_HEADS * ML_DV
CHUNK = 128
CONV_W = 4
EPS = 1e-6
NEG = -1e30
LOG2E = 1.4426950408889634

SPLIT_SIZES = (FOX_W, FOX_W, FOX_W, FOX_HEADS, FOX_W, 2 * ML_HEADS * ML_DK, ML_W,
               ML_HEADS, ML_HEADS, ML_W, ML_W, D_MODEL, D_MODEL)
SPLIT_POINTS = tuple(int(p) for p in np.cumsum(SPLIT_SIZES)[:-1])

Z_FQ, Z_FK, Z_FV, Z_FZ = 0, 1024, 2048, 3072
Z_MQ, Z_MK, Z_MV, Z_MO, Z_MZ = 4096, 5120, 6144, 8192, 10240
Z_GA, Z_GB = 12288, 13312
NZ = 14336
G_FOX, G_LI, G_LF = 0, 8, 12

VMEM_LIMIT = 48 * 1024 * 1024


def _largest_divisor(n, candidates):
    for c in candidates:
        if n % c == 0:
            return c
    raise ValueError(f"no tile in {candidates} divides {n}")


def _inproj_kernel(x_ref, g_ref, w_ref, wg_ref, z_ref, zg_ref, xn_ref):
    @pl.when(pl.program_id(1) == 0)
    def _():
        x = x_ref[...]
        ms = jnp.mean(x * x, axis=-1, keepdims=True)
        xn = (x * lax.rsqrt(ms + EPS) * g_ref[...]).astype(BF16)
        xn_ref[...] = xn
        zg_ref[...] = jnp.dot(xn, wg_ref[...], preferred_element_type=F32)

    z_ref[...] = jnp.dot(xn_ref[...], w_ref[...],
                         preferred_element_type=F32).astype(z_ref.dtype)


def _inproj(x2, g, wz, wg):
    t = x2.shape[0]
    tm = _largest_divisor(t, (1024, 512, 256, 128))
    tn = 512
    return pl.pallas_call(
        _inproj_kernel,
        out_shape=(jax.ShapeDtypeStruct((t, NZ), BF16),
                   jax.ShapeDtypeStruct((t, 128), F32)),
        grid=(t // tm, NZ // tn),
        in_specs=[pl.BlockSpec((tm, D_MODEL), lambda i, j: (i, 0)),
                  pl.BlockSpec((1, D_MODEL), lambda i, j: (0, 0)),
                  pl.BlockSpec((D_MODEL, tn), lambda i, j: (0, j)),
                  pl.BlockSpec((D_MODEL, 128), lambda i, j: (0, 0))],
        out_specs=(pl.BlockSpec((tm, tn), lambda i, j: (i, j)),
                   pl.BlockSpec((tm, 128), lambda i, j: (i, 0))),
        scratch_shapes=[pltpu.VMEM((tm, D_MODEL), BF16)],
        compiler_params=pltpu.CompilerParams(
            dimension_semantics=("parallel", "arbitrary"),
            vmem_limit_bytes=VMEM_LIMIT),
        name="inproj",
    )(x2, g, wz, wg)


def _gates_kernel(zg_ref, bias_ref, gcol_ref, grow_ref, carry_ref):
    c = pl.program_id(1)

    @pl.when(c == 0)
    def _():
        carry_ref[...] = jnp.zeros_like(carry_ref)

    a = zg_ref[0] + bias_ref[...]
    log_sig = jnp.minimum(a, 0.0) - jnp.log1p(jnp.exp(-jnp.abs(a)))
    row = lax.broadcasted_iota(jnp.int32, (CHUNK, 128), 0)
    lane = lax.broadcasted_iota(jnp.int32, (CHUNK, 128), 1)
    is_fox = lane < G_LI
    is_li = (lane >= G_LI) & (lane < G_LF)
    is_lf = (lane >= G_LF) & (lane < G_LF + ML_HEADS)
    real = (c > 0) | (row >= N_ZERO)
    v = jnp.where((is_fox | is_lf) & real, log_sig, 0.0)

    tri = (row >= lane).astype(BF16)
    hi = v.astype(BF16)
    r1 = v - hi.astype(F32)
    mid = r1.astype(BF16)
    lo = (r1 - mid.astype(F32)).astype(BF16)
    cum = (jnp.dot(tri, hi, preferred_element_type=F32)
           + jnp.dot(tri, mid, preferred_element_type=F32)
           + jnp.dot(tri, lo, preferred_element_type=F32))
    cum = cum + carry_ref[...]
    carry_ref[...] = jnp.where(lane[:1] < G_LI, cum[CHUNK - 1:CHUNK, :], 0.0)

    log_i = jnp.where(real, a, NEG)
    out = jnp.where(is_li, log_i, cum)
    gcol_ref[0] = out
    grow_ref[0] = out.T[:16, :]


def _gates(zg3, bias):
    b, lp, _ = zg3.shape
    return pl.pallas_call(
        _gates_kernel,
        out_shape=(jax.ShapeDtypeStruct((b, lp, 128), F32),
                   jax.ShapeDtypeStruct((b, 16, lp), F32)),
        grid=(b, lp // CHUNK),
        in_specs=[pl.BlockSpec((1, CHUNK, 128), lambda i, c: (i, c, 0)),
                  pl.BlockSpec((1, 128), lambda i, c: (0, 0))],
        out_specs=(pl.BlockSpec((1, CHUNK, 128), lambda i, c: (i, c, 0)),
                   pl.BlockSpec((1, 16, CHUNK), lambda i, c: (i, 0, c))),
        scratch_shapes=[pltpu.VMEM((1, 128), F32)],
        compiler_params=pltpu.CompilerParams(
            dimension_semantics=("parallel", "arbitrary")),
        name="gates",
    )(zg3, bias)


def _fox_kernel(q_ref, k_ref, v_ref, fz_ref, gcol_ref, o_ref, kx_ref, vt_ref, *, tq, nq):
    h = pl.program_id(1)
    lp = k_ref.shape[1]
    r128 = lax.broadcasted_iota(jnp.int32, (128, 128), 0)
    c128 = lax.broadcasted_iota(jnp.int32, (128, 128), 1)
    sel = [((r128 == G_FOX + h) & (c128 == j)).astype(BF16) for j in range(3)]

    for c in range(lp // 128):
        r0 = c * 128
        kx_ref[r0:r0 + 128, 0:FOX_DH] = k_ref[0, r0:r0 + 128, :]
        g = gcol_ref[0, r0:r0 + 128, :] * (-LOG2E)
        hi = g.astype(BF16)
        r1 = g - hi.astype(F32)
        mid = r1.astype(BF16)
        lo = (r1 - mid.astype(F32)).astype(BF16)
        aug = (jnp.dot(hi, sel[0], preferred_element_type=F32)
               + jnp.dot(mid, sel[1], preferred_element_type=F32)
               + jnp.dot(lo, sel[2], preferred_element_type=F32))
        kx_ref[r0:r0 + 128, FOX_DH:2 * FOX_DH] = aug.astype(BF16)
        vt_ref[:, r0:r0 + 128] = v_ref[0, r0:r0 + 128, :].astype(F32).T.astype(BF16)

    def extend_q(q):
        lane = lax.broadcasted_iota(jnp.int32, q.shape, 1)
        return jnp.concatenate([q, jnp.where(lane < 3, 1.0, 0.0).astype(BF16)], axis=1)

    def tile(qx, ks, tk, mask_fn):
        s = lax.dot_general(kx_ref[ks:ks + tk, :], qx, (((1,), (1,)), ((), ())),
                            preferred_element_type=F32)
        if mask_fn is not None:
            key = lax.broadcasted_iota(jnp.int32, s.shape, 0)
            qry = lax.broadcasted_iota(jnp.int32, s.shape, 1)
            s = jnp.where(mask_fn(key, qry), s, NEG)
        m = jnp.max(s, axis=0, keepdims=True)
        p = jnp.exp2(s - m)
        l = jnp.sum(p, axis=0, keepdims=True)
        acc = jnp.dot(vt_ref[:, ks:ks + tk], p.astype(BF16), preferred_element_type=F32)
        return m, l, acc

    def finish(parts, q0, n):
        m = functools.reduce(jnp.maximum, [pm for pm, _, _ in parts])
        l = 0.0
        acc = 0.0
        for pm, pl_, pacc in parts:
            w = jnp.exp2(pm - m)
            l = l + w * pl_
            acc = acc + w * pacc
        fz = fz_ref[0, q0:q0 + n, :].astype(F32)
        o = (acc / l).T * (fz * jax.nn.sigmoid(fz))
        o_ref[0, q0:q0 + n, :] = o.astype(o_ref.dtype)

    qx = extend_q(q_ref[0, 0:PAD, :])
    finish([tile(qx, 0, PAD, lambda key, qry: (key >= N_ZERO) & (key <= qry))], 0, PAD)

    for i in range(nq):
        q0 = PAD + i * tq
        qx = extend_q(q_ref[0, q0:q0 + tq, :])
        parts = [tile(qx, 0, PAD, lambda key, qry: key >= N_ZERO)]
        parts += [tile(qx, PAD + j * tq, tq, None) for j in range(i)]
        parts.append(tile(qx, q0, tq, lambda key, qry: key <= qry))
        finish(parts, q0, tq)


def _fox(z3, gcol):
    b, lp, _ = z3.shape
    n_real = lp - PAD
    tq = _largest_divisor(n_real, (512, 256, 128))
    blk = lambda off: pl.BlockSpec((1, lp, FOX_DH),
                                   lambda i, h, off=off: (i, 0, off // FOX_DH + h))
    return pl.pallas_call(
        functools.partial(_fox_kernel, tq=tq, nq=n_real // tq),
        out_shape=jax.ShapeDtypeStruct((b, lp, FOX_W), BF16),
        grid=(b, FOX_HEADS),
        in_specs=[blk(Z_FQ), blk(Z_FK), blk(Z_FV), blk(Z_FZ),
                  pl.BlockSpec((1, lp, 128), lambda i, h: (i, 0, 0))],
        out_specs=pl.BlockSpec((1, lp, FOX_DH), lambda i, h: (i, 0, h)),
        scratch_shapes=[pltpu.VMEM((lp, 2 * FOX_DH), BF16),
                        pltpu.VMEM((FOX_DH, lp), BF16)],
        compiler_params=pltpu.CompilerParams(
            dimension_semantics=("parallel", "arbitrary"),
            vmem_limit_bytes=VMEM_LIMIT),
        name="fox",
    )(z3, z3, z3, z3, gcol)


def _mlstm_kernel(mq_ref, mk_ref, mv_ref, mo_ref, mz_ref, gcol_ref, grow_ref,
                  cw_ref, cb_ref, gh_ref, o_ref,
                  xbuf_ref, c_ref, n_ref, m_ref):
    c = pl.program_id(1)

    @pl.when(c == 0)
    def _():
        xbuf_ref[0:8, :] = jnp.zeros((8, 2 * ML_HEADS * ML_DK), F32)
        c_ref[...] = jnp.zeros_like(c_ref)
        n_ref[...] = jnp.zeros_like(n_ref)
        m_ref[...] = jnp.zeros_like(m_ref)

    half = ML_HEADS * ML_DK
    xbuf_ref[8:8 + CHUNK, 0:half] = mq_ref[0].astype(F32)
    xbuf_ref[8:8 + CHUNK, half:2 * half] = mk_ref[0].astype(F32)
    y = cb_ref[...]
    for j in range(CONV_W):
        y = y + cw_ref[j:j + 1, :] * xbuf_ref[pl.ds(8 - (CONV_W - 1) + j, CHUNK), :]
    xbuf_ref[0:8, :] = xbuf_ref[CHUNK:CHUNK + 8, :]
    qk = y * jax.nn.sigmoid(y)

    gcol = gcol_ref[0]
    grow = grow_ref[0]
    rows = lax.broadcasted_iota(jnp.int32, (CHUNK, CHUNK), 0)
    cols = lax.broadcasted_iota(jnp.int32, (CHUNK, CHUNK), 1)
    causal = cols <= rows

    for hd in range(ML_HEADS):
        q = (qk[:, hd * ML_DK:(hd + 1) * ML_DK] * (ML_DK ** -0.5)).astype(BF16)
        k = qk[:, half + hd * ML_DK: half + (hd + 1) * ML_DK]
        v = mv_ref[0, :, hd * ML_DV:(hd + 1) * ML_DV]
        b_col = gcol[:, G_LF + hd:G_LF + hd + 1]
        li_col = gcol[:, G_LI + hd:G_LI + hd + 1]
        b_row = grow[G_LF + hd:G_LF + hd + 1, :]
        li_row = grow[G_LI + hd:G_LI + hd + 1, :]
        m_prev = m_ref[hd:hd + 1, 0:1]
        c_prev = c_ref[hd]
        n_prev = n_ref[hd:hd + 1, :]

        log_d = jnp.where(causal, b_col - b_row + li_row, NEG)
        log_inter = b_col + m_prev
        m_t = jnp.maximum(log_inter, jnp.max(log_d, axis=-1, keepdims=True))
        d = jnp.exp(log_d - m_t)
        inter = jnp.exp(log_inter - m_t)
        s = lax.dot_general(q, k.astype(BF16), (((1,), (1,)), ((), ())),
                            preferred_element_type=F32) * d
        num = (inter * jnp.dot(q, c_prev.astype(BF16), preferred_element_type=F32)
               + jnp.dot(s.astype(BF16), v, preferred_element_type=F32))
        den = (inter * jnp.sum(q.astype(F32) * n_prev, axis=-1, keepdims=True)
               + jnp.sum(s, axis=-1, keepdims=True))
        hh = num / jnp.maximum(jnp.abs(den), jnp.exp(-m_t))

        b_last = b_col[CHUNK - 1:CHUNK, :]
        log_w = b_last - b_col + li_col
        m_new = jnp.maximum(b_last + m_prev, jnp.max(log_w, axis=0, keepdims=True))
        w = jnp.exp(log_w - m_new)
        decay = jnp.exp(b_last + m_prev - m_new)
        kw = k * w
        c_ref[hd] = decay * c_prev + lax.dot_general(
            kw.astype(BF16), v, (((0,), (0,)), ((), ())), preferred_element_type=F32)
        n_ref[hd:hd + 1, :] = decay * n_prev + jnp.sum(kw, axis=0, keepdims=True)
        m_ref[hd:hd + 1, :] = jnp.broadcast_to(m_new, (1, 128))

        mo = mo_ref[0, :, hd * ML_DV:(hd + 1) * ML_DV].astype(F32)
        mz = mz_ref[0, :, hd * ML_DV:(hd + 1) * ML_DV].astype(F32)
        hb = jax.nn.sigmoid(mo) * hh
        ms = jnp.mean(hb * hb, axis=-1, keepdims=True)
        hb = hb * lax.rsqrt(ms + EPS) * gh_ref[:, hd * ML_DV:(hd + 1) * ML_DV]
        o_ref[0, :, hd * ML_DV:(hd + 1) * ML_DV] = (
            hb * (mz * jax.nn.sigmoid(mz))).astype(o_ref.dtype)


def _mlstm(z3, gcol, grow, conv_w, conv_b, g_head):
    b, lp, _ = z3.shape
    half = ML_HEADS * ML_DK
    zblk = lambda width, off: pl.BlockSpec(
        (1, CHUNK, width), lambda i, c, off=off, width=width: (i, c, off // width))
    full = lambda shape: pl.BlockSpec(shape, lambda i, c: (0,) * len(shape))
    return pl.pallas_call(
        _mlstm_kernel,
        out_shape=jax.ShapeDtypeStruct((b, lp, ML_W), BF16),
        grid=(b, lp // CHUNK),
        in_specs=[zblk(half, Z_MQ), zblk(half, Z_MK), zblk(ML_W, Z_MV),
                  zblk(ML_W, Z_MO), zblk(ML_W, Z_MZ),
                  pl.BlockSpec((1, CHUNK, 128), lambda i, c: (i, c, 0)),
                  pl.BlockSpec((1, 16, CHUNK), lambda i, c: (i, 0, c)),
                  full((CONV_W, 2 * half)), full((1, 2 * half)), full((1, ML_W))],
        out_specs=pl.BlockSpec((1, CHUNK, ML_W), lambda i, c: (i, c, 0)),
        scratch_shapes=[pltpu.VMEM((8 + CHUNK, 2 * half), F32),
                        pltpu.VMEM((ML_HEADS, ML_DK, ML_DV), F32),
                        pltpu.VMEM((ML_HEADS, ML_DK), F32),
                        pltpu.VMEM((ML_HEADS, 128), F32)],
        compiler_params=pltpu.CompilerParams(
            dimension_semantics=("parallel", "arbitrary"),
            vmem_limit_bytes=VMEM_LIMIT),
        name="mlstm",
    )(z3, z3, z3, z3, z3, gcol, grow, conv_w, conv_b, g_head)


def _outproj_kernel(oa_ref, hb_ref, ga_ref, gb_ref, x_ref, wa_ref, wb_ref, wo_ref,
                    g_ref, o_ref):
    ya = jnp.dot(oa_ref[...], wa_ref[...], preferred_element_type=F32)
    yb = jnp.dot(hb_ref[...], wb_ref[...], preferred_element_type=F32)
    merged = (jax.nn.sigmoid(ga_ref[...].astype(F32)) * ya
              + jax.nn.sigmoid(gb_ref[...].astype(F32)) * yb)
    y = jnp.dot(merged.astype(BF16), wo_ref[...], preferred_element_type=F32)
    ms = jnp.mean(y * y, axis=-1, keepdims=True)
    o_ref[...] = x_ref[...] + y * lax.rsqrt(ms + EPS) * g_ref[...]


def _outproj(oa2, hb2, z2, x2, wa, wb, wo, g):
    t = x2.shape[0]
    tm = _largest_divisor(t, (512, 256, 128))
    rows = lambda width, col=0: pl.BlockSpec((tm, width), lambda i, col=col: (i, col))
    full = lambda shape: pl.BlockSpec(shape, lambda i: (0,) * len(shape))
    return pl.pallas_call(
        _outproj_kernel,
        out_shape=jax.ShapeDtypeStruct((t, D_MODEL), F32),
        grid=(t // tm,),
        in_specs=[rows(FOX_W), rows(ML_W),
                  rows(D_MODEL, Z_GA // D_MODEL), rows(D_MODEL, Z_GB // D_MODEL),
                  rows(D_MODEL),
                  full((FOX_W, D_MODEL)), full((ML_W, D_MODEL)), full((D_MODEL, D_MODEL)),
                  full((1, D_MODEL))],
        out_specs=rows(D_MODEL),
        compiler_params=pltpu.CompilerParams(
            dimension_semantics=("parallel",),
            vmem_limit_bytes=VMEM_LIMIT),
        name="outproj",
    )(oa2, hb2, z2, z2, x2, wa, wb, wo, g)


def _layer(h, g_pre, g_post, w_in, b_fox_f, conv_w, conv_b, b_i, b_f, g_head, w_a, w_b, w_o):
    b, lp, d = h.shape
    (fq, fk, fv, ff, fz, mqk, mv, mi, mf, mo, mz, ga, gb) = jnp.split(w_in, SPLIT_POINTS, axis=1)
    wz = jnp.concatenate([fq * (LOG2E * FOX_DH ** -0.5), fk, fv, fz, mqk, mv, mo, mz, ga, gb],
                         axis=1).astype(BF16)
    wg = jnp.concatenate([ff, mi, mf, jnp.zeros((d, 128 - 16), F32)], axis=1).astype(BF16)
    bias = jnp.concatenate([b_fox_f, b_i, b_f, jnp.zeros((128 - 16,), F32)])[None, :]

    x2 = h.reshape(b * lp, d)
    z2, zg2 = _inproj(x2, g_pre[None, :], wz, wg)
    z3 = z2.reshape(b, lp, NZ)
    gcol, grow = _gates(zg2.reshape(b, lp, 128), bias)
    oa = _fox(z3, gcol)
    hb = _mlstm(z3, gcol, grow, conv_w, conv_b[None, :], g_head[None, :])
    out = _outproj(oa.reshape(b * lp, FOX_W), hb.reshape(b * lp, ML_W), z2, x2,
                   w_a.astype(BF16), w_b.astype(BF16), w_o.astype(BF16), g_post[None, :])
    return out.reshape(b, lp, d)


def kernel(x, meta_tokens, norm_pre, norm_post, w_in, b_fox_f, conv_w, conv_b,
           b_mlstm_i, b_mlstm_f, mlstm_head_norm, w_a, w_b, w_o):
    b, seq, d = x.shape
    assert d == D_MODEL and seq % CHUNK == 0
    assert meta_tokens.shape == (N_META, D_MODEL)
    lead = jnp.concatenate([jnp.zeros((N_ZERO, d), x.dtype), meta_tokens.astype(x.dtype)], axis=0)
    h = jnp.concatenate([jnp.broadcast_to(lead[None], (b, PAD, d)), x], axis=1)
    for l in range(norm_pre.shape[0]):
        h = _layer(h, norm_pre[l], norm_post[l], w_in[l], b_fox_f[l], conv_w[l], conv_b[l],
                   b_mlstm_i[l], b_mlstm_f[l], mlstm_head_norm[l], w_a[l], w_b[l], w_o[l])
    return h[:, PAD:]
```

```python
import functools

import jax
import jax.numpy as jnp
import numpy as np
from jax import lax
from jax.experimental import pallas as pl
from jax.experimental.pallas import tpu as pltpu

F32 = jnp.float32
BF16 = jnp.bfloat16

D_MODEL = 1024
N_META = 16
PAD = 128
N_ZERO = PAD - N_META
FOX_HEADS = 8
FOX_DH = 128
FOX_W = FOX_HEADS * FOX_DH
ML_HEADS = 4
ML_DK = 256
ML_DV = 512
ML_W = ML_HEADS * ML_DV
CHUNK = 128
CONV_W = 4
EPS = 1e-6
NEG = -1e30
LOG2E = 1.4426950408889634

SPLIT_SIZES = (FOX_W, FOX_W, FOX_W, FOX_HEADS, FOX_W, 2 * ML_HEADS * ML_DK, ML_W,
               ML_HEADS, ML_HEADS, ML_W, ML_W, D_MODEL, D_MODEL)
SPLIT_POINTS = tuple(int(p) for p in np.cumsum(SPLIT_SIZES)[:-1])

Z_FQ, Z_FK, Z_FV, Z_FZ = 0, 1024, 2048, 3072
Z_MQ, Z_MK, Z_MV, Z_MO, Z_MZ = 4096, 5120, 6144, 8192, 10240
Z_GA, Z_GB = 12288, 13312
NZ = 14336
G_FOX, G_LI, G_LF = 0, 8, 12

VMEM_LIMIT = 48 * 1024 * 1024


def _largest_divisor(n, candidates):
    for c in candidates:
        if n % c == 0:
            return c
    raise ValueError(f"no tile in {candidates} divides {n}")


def _inproj_kernel(x_ref, g_ref, w_ref, wg_ref, z_ref, zg_ref, xn_ref, *, n_sub):
    @pl.when(pl.program_id(1) == 0)
    def _():
        x = x_ref[...]
        ms = jnp.mean(x * x, axis=-1, keepdims=True)
        xn = (x * lax.rsqrt(ms + EPS) * g_ref[...]).astype(BF16)
        xn_ref[...] = xn
        zg_ref[...] = jnp.dot(xn, wg_ref[...], preferred_element_type=F32)

    xn = xn_ref[...]
    ts = w_ref.shape[1] // n_sub
    for n in range(n_sub):
        z_ref[:, n * ts:(n + 1) * ts] = jnp.dot(
            xn, w_ref[:, n * ts:(n + 1) * ts], preferred_element_type=F32).astype(z_ref.dtype)


def _inproj(x2, g, wz, wg):
    t = x2.shape[0]
    tm = _largest_divisor(t, (1024, 512, 256, 128))
    tn, n_sub = 2048, 4
    return pl.pallas_call(
        functools.partial(_inproj_kernel, n_sub=n_sub),
        out_shape=(jax.ShapeDtypeStruct((t, NZ), BF16),
                   jax.ShapeDtypeStruct((t, 128), F32)),
        grid=(t // tm, NZ // tn),
        in_specs=[pl.BlockSpec((tm, D_MODEL), lambda i, j: (i, 0)),
                  pl.BlockSpec((1, D_MODEL), lambda i, j: (0, 0)),
                  pl.BlockSpec((D_MODEL, tn), lambda i, j: (0, j)),
                  pl.BlockSpec((D_MODEL, 128), lambda i, j: (0, 0))],
        out_specs=(pl.BlockSpec((tm, tn), lambda i, j: (i, j)),
                   pl.BlockSpec((tm, 128), lambda i, j: (i, 0))),
        scratch_shapes=[pltpu.VMEM((tm, D_MODEL), BF16)],
        compiler_params=pltpu.CompilerParams(
            dimension_semantics=("parallel", "arbitrary"),
            vmem_limit_bytes=VMEM_LIMIT),
        name="inproj",
    )(x2, g, wz, wg)


def _gates_kernel(zg_ref, bias_ref, gcol_ref, grow_ref):
    row = lax.broadcasted_iota(jnp.int32, (CHUNK, 128), 0)
    lane = lax.broadcasted_iota(jnp.int32, (CHUNK, 128), 1)
    is_fox = lane < G_LI
    is_li = (lane >= G_LI) & (lane < G_LF)
    is_lf = (lane >= G_LF) & (lane < G_LF + ML_HEADS)
    tri = (row >= lane).astype(BF16)
    carry = jnp.zeros((1, 128), F32)
    for c in range(zg_ref.shape[1] // CHUNK):
        r0 = c * CHUNK
        a = zg_ref[0, r0:r0 + CHUNK, :] + bias_ref[...]
        log_sig = jnp.minimum(a, 0.0) - jnp.log1p(jnp.exp(-jnp.abs(a)))
        real = (row >= N_ZERO) if c == 0 else None
        v = jnp.where(is_fox | is_lf, log_sig, 0.0)
        if real is not None:
            v = jnp.where(real, v, 0.0)

        hi = v.astype(BF16)
        r1 = v - hi.astype(F32)
        mid = r1.astype(BF16)
        lo = (r1 - mid.astype(F32)).astype(BF16)
        cum = (jnp.dot(tri, hi, preferred_element_type=F32)
               + jnp.dot(tri, mid, preferred_element_type=F32)
               + jnp.dot(tri, lo, preferred_element_type=F32))
        cum = cum + carry
        carry = jnp.where(lane[:1] < G_LI, cum[CHUNK - 1:CHUNK, :], 0.0)

        log_i = a if real is None else jnp.where(real, a, NEG)
        out = jnp.where(is_li, log_i, cum)
        gcol_ref[0, r0:r0 + CHUNK, :] = out
        grow_ref[0, :, r0:r0 + CHUNK] = out.T[:16, :]


def _gates(zg3, bias):
    b, lp, _ = zg3.shape
    return pl.pallas_call(
        _gates_kernel,
        out_shape=(jax.ShapeDtypeStruct((b, lp, 128), F32),
                   jax.ShapeDtypeStruct((b, 16, lp), F32)),
        grid=(b,),
        in_specs=[pl.BlockSpec((1, lp, 128), lambda i: (i, 0, 0)),
                  pl.BlockSpec((1, 128), lambda i: (0, 0))],
        out_specs=(pl.BlockSpec((1, lp, 128), lambda i: (i, 0, 0)),
                   pl.BlockSpec((1, 16, lp), lambda i: (i, 0, 0))),
        compiler_params=pltpu.CompilerParams(dimension_semantics=("parallel",)),
        name="gates",
    )(zg3, bias)


def _fox_kernel(q_ref, k_ref, v_ref, fz_ref, gcol_ref, o_ref, kx_ref, vt_ref, *, tq, nq):
    h = pl.program_id(1)
    lp = k_ref.shape[1]
    r128 = lax.broadcasted_iota(jnp.int32, (128, 128), 0)
    c128 = lax.broadcasted_iota(jnp.int32, (128, 128), 1)
    sel = [((r128 == G_FOX + h) & (c128 == j)).astype(BF16) for j in range(3)]

    for c in range(lp // 128):
        r0 = c * 128
        kx_ref[r0:r0 + 128, 0:FOX_DH] = k_ref[0, r0:r0 + 128, :]
        g = gcol_ref[0, r0:r0 + 128, :] * (-LOG2E)
        hi = g.astype(BF16)
        r1 = g - hi.astype(F32)
        mid = r1.astype(BF16)
        lo = (r1 - mid.astype(F32)).astype(BF16)
        aug = (jnp.dot(hi, sel[0], preferred_element_type=F32)
               + jnp.dot(mid, sel[1], preferred_element_type=F32)
               + jnp.dot(lo, sel[2], preferred_element_type=F32))
        kx_ref[r0:r0 + 128, FOX_DH:2 * FOX_DH] = aug.astype(BF16)
        vt_ref[:, r0:r0 + 128] = v_ref[0, r0:r0 + 128, :].astype(F32).T.astype(BF16)

    def extend_q(q):
        lane = lax.broadcasted_iota(jnp.int32, q.shape, 1)
        return jnp.concatenate([q, jnp.where(lane < 3, 1.0, 0.0).astype(BF16)], axis=1)

    def tile(qx, ks, tk, mask_fn):
        s = lax.dot_general(kx_ref[ks:ks + tk, :], qx, (((1,), (1,)), ((), ())),
                            preferred_element_type=F32)
        if mask_fn is not None:
            key = lax.broadcasted_iota(jnp.int32, s.shape, 0)
            qry = lax.broadcasted_iota(jnp.int32, s.shape, 1)
            s = jnp.where(mask_fn(key, qry), s, NEG)
        m = jnp.max(s, axis=0, keepdims=True)
        p = jnp.exp2(s - m)
        l = jnp.sum(p, axis=0, keepdims=True)
        acc = jnp.dot(vt_ref[:, ks:ks + tk], p.astype(BF16), preferred_element_type=F32)
        return m, l, acc

    def finish(parts, q0, n):
        m = functools.reduce(jnp.maximum, [pm for pm, _, _ in parts])
        l = 0.0
        acc = 0.0
        for pm, pl_, pacc in parts:
            w = jnp.exp2(pm - m)
            l = l + w * pl_
            acc = acc + w * pacc
        fz = fz_ref[0, q0:q0 + n, :].astype(F32)
        o = (acc / l).T * (fz * jax.nn.sigmoid(fz))
        o_ref[0, q0:q0 + n, :] = o.astype(o_ref.dtype)

    qx = extend_q(q_ref[0, 0:PAD, :])
    finish([tile(qx, 0, PAD, lambda key, qry: (key >= N_ZERO) & (key <= qry))], 0, PAD)

    for i in range(nq):
        q0 = PAD + i * tq
        qx = extend_q(q_ref[0, q0:q0 + tq, :])
        parts = [tile(qx, 0, PAD, lambda key, qry: key >= N_ZERO)]
        parts += [tile(qx, PAD + j * tq, tq, None) for j in range(i)]
        parts.append(tile(qx, q0, tq, lambda key, qry: key <= qry))
        finish(parts, q0, tq)


def _fox(z3, gcol):
    b, lp, _ = z3.shape
    n_real = lp - PAD
    tq = _largest_divisor(n_real, (512, 256, 128))
    blk = lambda off: pl.BlockSpec((1, lp, FOX_DH),
                                   lambda i, h, off=off: (i, 0, off // FOX_DH + h))
    return pl.pallas_call(
        functools.partial(_fox_kernel, tq=tq, nq=n_real // tq),
        out_shape=jax.ShapeDtypeStruct((b, lp, FOX_W), BF16),
        grid=(b, FOX_HEADS),
        in_specs=[blk(Z_FQ), blk(Z_FK), blk(Z_FV), blk(Z_FZ),
                  pl.BlockSpec((1, lp, 128), lambda i, h: (i, 0, 0))],
        out_specs=pl.BlockSpec((1, lp, FOX_DH), lambda i, h: (i, 0, h)),
        scratch_shapes=[pltpu.VMEM((lp, 2 * FOX_DH), BF16),
                        pltpu.VMEM((FOX_DH, lp), BF16)],
        compiler_params=pltpu.CompilerParams(
            dimension_semantics=("parallel", "arbitrary"),
            vmem_limit_bytes=VMEM_LIMIT),
        name="fox",
    )(z3, z3, z3, z3, gcol)


def _mlstm_kernel(mq_ref, mk_ref, mv_ref, mo_ref, mz_ref, gcol_ref, grow_ref,
                  cw_ref, cb_ref, gh_ref, o_ref,
                  xbuf_ref, c_ref, n_ref, m_ref):
    c = pl.program_id(1)

    @pl.when(c == 0)
    def _():
        xbuf_ref[0:8, :] = jnp.zeros((8, 2 * ML_HEADS * ML_DK), F32)
        c_ref[...] = jnp.zeros_like(c_ref)
        n_ref[...] = jnp.zeros_like(n_ref)
        m_ref[...] = jnp.zeros_like(m_ref)

    half = ML_HEADS * ML_DK
    xbuf_ref[8:8 + CHUNK, 0:half] = mq_ref[0].astype(F32)
    xbuf_ref[8:8 + CHUNK, half:2 * half] = mk_ref[0].astype(F32)
    y = cb_ref[...]
    for j in range(CONV_W):
        y = y + cw_ref[j:j + 1, :] * xbuf_ref[pl.ds(8 - (CONV_W - 1) + j, CHUNK), :]
    xbuf_ref[0:8, :] = xbuf_ref[CHUNK:CHUNK + 8, :]
    qk = y * jax.nn.sigmoid(y)

    gcol = gcol_ref[0]
    grow = grow_ref[0]
    rows = lax.broadcasted_iota(jnp.int32, (CHUNK, CHUNK), 0)
    cols = lax.broadcasted_iota(jnp.int32, (CHUNK, CHUNK), 1)
    causal = cols <= rows

    for hd in range(ML_HEADS):
        q = (qk[:, hd * ML_DK:(hd + 1) * ML_DK] * (ML_DK ** -0.5)).astype(BF16)
        k = qk[:, half + hd * ML_DK: half + (hd + 1) * ML_DK]
        v = mv_ref[0, :, hd * ML_DV:(hd + 1) * ML_DV]
        b_col = gcol[:, G_LF + hd:G_LF + hd + 1]
        li_col = gcol[:, G_LI + hd:G_LI + hd + 1]
        b_row = grow[G_LF + hd:G_LF + hd + 1, :]
        li_row = grow[G_LI + hd:G_LI + hd + 1, :]
        m_prev = m_ref[hd:hd + 1, 0:1]
        c_prev = c_ref[hd]
        n_prev = n_ref[hd:hd + 1, :]

        log_d = jnp.where(causal, b_col - b_row + li_row, NEG)
        log_inter = b_col + m_prev
        m_t = jnp.maximum(log_inter, jnp.max(log_d, axis=-1, keepdims=True))
        d = jnp.exp(log_d - m_t)
        inter = jnp.exp(log_inter - m_t)
        s = lax.dot_general(q, k.astype(BF16), (((1,), (1,)), ((), ())),
                            preferred_element_type=F32) * d
        num = (inter * jnp.dot(q, c_prev.astype(BF16), preferred_element_type=F32)
               + jnp.dot(s.astype(BF16), v, preferred_element_type=F32))
        den = (inter * jnp.sum(q.astype(F32) * n_prev, axis=-1, keepdims=True)
               + jnp.sum(s, axis=-1, keepdims=True))
        hh = num / jnp.maximum(jnp.abs(den), jnp.exp(-m_t))

        b_last = b_col[CHUNK - 1:CHUNK, :]
        log_w = b_last - b_col + li_col
        m_new = jnp.maximum(b_last + m_prev, jnp.max(log_w, axis=0, keepdims=True))
        w = jnp.exp(log_w - m_new)
        decay = jnp.exp(b_last + m_prev - m_new)
        kw = k * w
        c_ref[hd] = decay * c_prev + lax.dot_general(
            kw.astype(BF16), v, (((0,), (0,)), ((), ())), preferred_element_type=F32)
        n_ref[hd:hd + 1, :] = decay * n_prev + jnp.sum(kw, axis=0, keepdims=True)
        m_ref[hd:hd + 1, :] = jnp.broadcast_to(m_new, (1, 128))

        mo = mo_ref[0, :, hd * ML_DV:(hd + 1) * ML_DV].astype(F32)
        mz = mz_ref[0, :, hd * ML_DV:(hd + 1) * ML_DV].astype(F32)
        hb = jax.nn.sigmoid(mo) * hh
        ms = jnp.mean(hb * hb, axis=-1, keepdims=True)
        hb = hb * lax.rsqrt(ms + EPS) * gh_ref[:, hd * ML_DV:(hd + 1) * ML_DV]
        o_ref[0, :, hd * ML_DV:(hd + 1) * ML_DV] = (
            hb * (mz * jax.nn.sigmoid(mz))).astype(o_ref.dtype)


def _mlstm(z3, gcol, grow, conv_w, conv_b, g_head):
    b, lp, _ = z3.shape
    half = ML_HEADS * ML_DK
    zblk = lambda width, off: pl.BlockSpec(
        (1, CHUNK, width), lambda i, c, off=off, width=width: (i, c, off // width))
    full = lambda shape: pl.BlockSpec(shape, lambda i, c: (0,) * len(shape))
    return pl.pallas_call(
        _mlstm_kernel,
        out_shape=jax.ShapeDtypeStruct((b, lp, ML_W), BF16),
        grid=(b, lp // CHUNK),
        in_specs=[zblk(half, Z_MQ), zblk(half, Z_MK), zblk(ML_W, Z_MV),
                  zblk(ML_W, Z_MO), zblk(ML_W, Z_MZ),
                  pl.BlockSpec((1, CHUNK, 128), lambda i, c: (i, c, 0)),
                  pl.BlockSpec((1, 16, CHUNK), lambda i, c: (i, 0, c)),
                  full((CONV_W, 2 * half)), full((1, 2 * half)), full((1, ML_W))],
        out_specs=pl.BlockSpec((1, CHUNK, ML_W), lambda i, c: (i, c, 0)),
        scratch_shapes=[pltpu.VMEM((8 + CHUNK, 2 * half), F32),
                        pltpu.VMEM((ML_HEADS, ML_DK, ML_DV), F32),
                        pltpu.VMEM((ML_HEADS, ML_DK), F32),
                        pltpu.VMEM((ML_HEADS, 128), F32)],
        compiler_params=pltpu.CompilerParams(
            dimension_semantics=("parallel", "arbitrary"),
            vmem_limit_bytes=VMEM_LIMIT),
        name="mlstm",
    )(z3, z3, z3, z3, z3, gcol, grow, conv_w, conv_b, g_head)


def _outproj_kernel(oa_ref, hb_ref, ga_ref, gb_ref, x_ref, wa_ref, wb_ref, wo_ref,
                    g_ref, o_ref):
    ya = jnp.dot(oa_ref[...], wa_ref[...], preferred_element_type=F32)
    yb = jnp.dot(hb_ref[...], wb_ref[...], preferred_element_type=F32)
    merged = (jax.nn.sigmoid(ga_ref[...].astype(F32)) * ya
              + jax.nn.sigmoid(gb_ref[...].astype(F32)) * yb)
    y = jnp.dot(merged.astype(BF16), wo_ref[...], preferred_element_type=F32)
    ms = jnp.mean(y * y, axis=-1, keepdims=True)
    o_ref[...] = x_ref[...] + y * lax.rsqrt(ms + EPS) * g_ref[...]


def _outproj(oa2, hb2, z2, x2, wa, wb, wo, g):
    t = x2.shape[0]
    tm = _largest_divisor(t, (512, 256, 128))
    rows = lambda width, col=0: pl.BlockSpec((tm, width), lambda i, col=col: (i, col))
    full = lambda shape: pl.BlockSpec(shape, lambda i: (0,) * len(shape))
    return pl.pallas_call(
        _outproj_kernel,
        out_shape=jax.ShapeDtypeStruct((t, D_MODEL), F32),
        grid=(t // tm,),
        in_specs=[rows(FOX_W), rows(ML_W),
                  rows(D_MODEL, Z_GA // D_MODEL), rows(D_MODEL, Z_GB // D_MODEL),
                  rows(D_MODEL),
                  full((FOX_W, D_MODEL)), full((ML_W, D_MODEL)), full((D_MODEL, D_MODEL)),
                  full((1, D_MODEL))],
        out_specs=rows(D_MODEL),
        compiler_params=pltpu.CompilerParams(
            dimension_semantics=("parallel",),
            vmem_limit_bytes=VMEM_LIMIT),
        name="outproj",
    )(oa2, hb2, z2, z2, x2, wa, wb, wo, g)


def _layer(h, g_pre, g_post, wz, wg, bias, conv_w, conv_b, g_head, w_a, w_b, w_o):
    b, lp, d = h.shape
    x2 = h.reshape(b * lp, d)
    z2, zg2 = _inproj(x2, g_pre[None, :], wz, wg)
    z3 = z2.reshape(b, lp, NZ)
    gcol, grow = _gates(zg2.reshape(b, lp, 128), bias[None, :])
    oa = _fox(z3, gcol)
    hb = _mlstm(z3, gcol, grow, conv_w, conv_b[None, :], g_head[None, :])
    out = _outproj(oa.reshape(b * lp, FOX_W), hb.reshape(b * lp, ML_W), z2, x2,
                   w_a, w_b, w_o, g_post[None, :])
    return out.reshape(b, lp, d)


def kernel(x, meta_tokens, norm_pre, norm_post, w_in, b_fox_f, conv_w, conv_b,
           b_mlstm_i, b_mlstm_f, mlstm_head_norm, w_a, w_b, w_o):
    b, seq, d = x.shape
    depth = norm_pre.shape[0]
    assert d == D_MODEL and seq % CHUNK == 0
    assert meta_tokens.shape == (N_META, D_MODEL)

    (fq, fk, fv, ff, fz, mqk, mv, mi, mf, mo, mz, ga, gb) = jnp.split(w_in, SPLIT_POINTS, axis=2)
    wz = jnp.concatenate([fq * (LOG2E * FOX_DH ** -0.5), fk, fv, fz, mqk, mv, mo, mz, ga, gb],
                         axis=2).astype(BF16)
    wg = jnp.concatenate([ff, mi, mf, jnp.zeros((depth, d, 128 - 16), F32)], axis=2).astype(BF16)
    bias = jnp.concatenate([b_fox_f, b_mlstm_i, b_mlstm_f, jnp.zeros((depth, 128 - 16), F32)], axis=1)
    w_a, w_b, w_o = w_a.astype(BF16), w_b.astype(BF16), w_o.astype(BF16)

    lead = jnp.concatenate([jnp.zeros((N_ZERO, d), x.dtype), meta_tokens.astype(x.dtype)], axis=0)
    h = jnp.concatenate([jnp.broadcast_to(lead[None], (b, PAD, d)), x], axis=1)
    for l in range(depth):
        h = _layer(h, norm_pre[l], norm_post[l], wz[l], wg[l], bias[l], conv_w[l], conv_b[l],
                   mlstm_head_norm[l], w_a[l], w_b[l], w_o[l])
    return h[:, PAD:]
```

```python
import functools

import jax
import jax.numpy as jnp
import numpy as np
from jax import lax
from jax.experimental import pallas as pl
from jax.experimental.pallas import tpu as pltpu

F32 = jnp.float32
BF16 = jnp.bfloat16

D_MODEL = 1024
N_META = 16
PAD = 128
N_ZERO = PAD - N_META
FOX_HEADS = 8
FOX_DH = 128
FOX_W = FOX_HEADS * FOX_DH
ML_HEADS = 4
ML_DK = 256
ML_DV = 512
ML_W = ML_HEADS * ML_DV
CHUNK = 128
CONV_W = 4
EPS = 1e-6
NEG = -1e30
LOG2E = 1.4426950408889634

SPLIT_SIZES = (FOX_W, FOX_W, FOX_W, FOX_HEADS, FOX_W, 2 * ML_HEADS * ML_DK, ML_W,
               ML_HEADS, ML_HEADS, ML_W, ML_W, D_MODEL, D_MODEL)
SPLIT_POINTS = tuple(int(p) for p in np.cumsum(SPLIT_SIZES)[:-1])

Z_FQ, Z_FK, Z_FV, Z_FZ = 0, 1024, 2048, 3072
Z_MQ, Z_MK, Z_MV, Z_MO, Z_MZ = 4096, 5120, 6144, 8192, 10240
Z_GA, Z_GB = 12288, 13312
NZ = 14336
G_FOX, G_LI, G_LF = 0, 8, 12

VMEM_LIMIT = 48 * 1024 * 1024


def _sigmoid(x):
    return 0.5 * jnp.tanh(0.5 * x) + 0.5


def _silu(x):
    h = 0.5 * x
    return h * jnp.tanh(h) + h


def _largest_divisor(n, candidates):
    for c in candidates:
        if n % c == 0:
            return c
    raise ValueError(f"no tile in {candidates} divides {n}")


def _inproj_kernel(x_ref, g_ref, w_ref, wg_ref, z_ref, zg_ref, xn_ref, *, n_sub):
    @pl.when(pl.program_id(1) == 0)
    def _():
        x = x_ref[...]
        ms = jnp.mean(x * x, axis=-1, keepdims=True)
        xn = (x * lax.rsqrt(ms + EPS) * g_ref[...]).astype(BF16)
        xn_ref[...] = xn
        zg_ref[...] = jnp.dot(xn, wg_ref[...], preferred_element_type=F32)

    xn = xn_ref[...]
    ts = w_ref.shape[1] // n_sub
    for n in range(n_sub):
        z_ref[:, n * ts:(n + 1) * ts] = jnp.dot(
            xn, w_ref[:, n * ts:(n + 1) * ts], preferred_element_type=F32).astype(z_ref.dtype)


def _inproj(x2, g, wz, wg):
    t = x2.shape[0]
    tm = _largest_divisor(t, (1024, 512, 256, 128))
    tn, n_sub = 2048, 4
    return pl.pallas_call(
        functools.partial(_inproj_kernel, n_sub=n_sub),
        out_shape=(jax.ShapeDtypeStruct((t, NZ), BF16),
                   jax.ShapeDtypeStruct((t, 128), F32)),
        grid=(t // tm, NZ // tn),
        in_specs=[pl.BlockSpec((tm, D_MODEL), lambda i, j: (i, 0)),
                  pl.BlockSpec((1, D_MODEL), lambda i, j: (0, 0)),
                  pl.BlockSpec((D_MODEL, tn), lambda i, j: (0, j)),
                  pl.BlockSpec((D_MODEL, 128), lambda i, j: (0, 0))],
        out_specs=(pl.BlockSpec((tm, tn), lambda i, j: (i, j)),
                   pl.BlockSpec((tm, 128), lambda i, j: (i, 0))),
        scratch_shapes=[pltpu.VMEM((tm, D_MODEL), BF16)],
        compiler_params=pltpu.CompilerParams(
            dimension_semantics=("parallel", "arbitrary"),
            vmem_limit_bytes=VMEM_LIMIT),
        name="inproj",
    )(x2, g, wz, wg)


def _gates_kernel(zg_ref, bias_ref, gcol_ref, grow_ref):
    row = lax.broadcasted_iota(jnp.int32, (CHUNK, 128), 0)
    lane = lax.broadcasted_iota(jnp.int32, (CHUNK, 128), 1)
    is_fox = lane < G_LI
    is_li = (lane >= G_LI) & (lane < G_LF)
    is_lf = (lane >= G_LF) & (lane < G_LF + ML_HEADS)
    tri = (row >= lane).astype(BF16)
    carry = jnp.zeros((1, 128), F32)
    for c in range(zg_ref.shape[1] // CHUNK):
        r0 = c * CHUNK
        a = zg_ref[0, r0:r0 + CHUNK, :] + bias_ref[...]
        log_sig = jnp.minimum(a, 0.0) - jnp.log1p(jnp.exp(-jnp.abs(a)))
        real = (row >= N_ZERO) if c == 0 else None
        v = jnp.where(is_fox | is_lf, log_sig, 0.0)
        if real is not None:
            v = jnp.where(real, v, 0.0)

        hi = v.astype(BF16)
        r1 = v - hi.astype(F32)
        mid = r1.astype(BF16)
        lo = (r1 - mid.astype(F32)).astype(BF16)
        cum = (jnp.dot(tri, hi, preferred_element_type=F32)
               + jnp.dot(tri, mid, preferred_element_type=F32)
               + jnp.dot(tri, lo, preferred_element_type=F32))
        cum = cum + carry
        carry = jnp.where(lane[:1] < G_LI, cum[CHUNK - 1:CHUNK, :], 0.0)

        log_i = a if real is None else jnp.where(real, a, NEG)
        out = jnp.where(is_li, log_i, cum)
        gcol_ref[0, r0:r0 + CHUNK, :] = out
        grow_ref[0, :, r0:r0 + CHUNK] = out.T[:16, :]


def _gates(zg3, bias):
    b, lp, _ = zg3.shape
    return pl.pallas_call(
        _gates_kernel,
        out_shape=(jax.ShapeDtypeStruct((b, lp, 128), F32),
                   jax.ShapeDtypeStruct((b, 16, lp), F32)),
        grid=(b,),
        in_specs=[pl.BlockSpec((1, lp, 128), lambda i: (i, 0, 0)),
                  pl.BlockSpec((1, 128), lambda i: (0, 0))],
        out_specs=(pl.BlockSpec((1, lp, 128), lambda i: (i, 0, 0)),
                   pl.BlockSpec((1, 16, lp), lambda i: (i, 0, 0))),
        compiler_params=pltpu.CompilerParams(dimension_semantics=("parallel",)),
        name="gates",
    )(zg3, bias)


def _fox_kernel(q_ref, k_ref, v_ref, fz_ref, gcol_ref, o_ref, kx_ref, vt_ref, *, tq, nq):
    h = pl.program_id(1)
    lp = k_ref.shape[1]
    r128 = lax.broadcasted_iota(jnp.int32, (128, 128), 0)
    c128 = lax.broadcasted_iota(jnp.int32, (128, 128), 1)
    sel = [((r128 == G_FOX + h) & (c128 == j)).astype(BF16) for j in range(3)]

    for c in range(lp // 128):
        r0 = c * 128
        kx_ref[r0:r0 + 128, 0:FOX_DH] = k_ref[0, r0:r0 + 128, :]
        g = gcol_ref[0, r0:r0 + 128, :] * (-LOG2E)
        hi = g.astype(BF16)
        r1 = g - hi.astype(F32)
        mid = r1.astype(BF16)
        lo = (r1 - mid.astype(F32)).astype(BF16)
        aug = (jnp.dot(hi, sel[0], preferred_element_type=F32)
               + jnp.dot(mid, sel[1], preferred_element_type=F32)
               + jnp.dot(lo, sel[2], preferred_element_type=F32))
        kx_ref[r0:r0 + 128, FOX_DH:2 * FOX_DH] = aug.astype(BF16)
        vt_ref[0:FOX_DH, r0:r0 + 128] = v_ref[0, r0:r0 + 128, :].astype(F32).T.astype(BF16)
    ones_row = lax.broadcasted_iota(jnp.int32, (16, lp), 0) == 0
    vt_ref[FOX_DH:FOX_DH + 16, :] = jnp.where(ones_row, 1.0, 0.0).astype(BF16)

    def extend_q(q):
        lane = lax.broadcasted_iota(jnp.int32, q.shape, 1)
        return jnp.concatenate([q, jnp.where(lane < 3, 1.0, 0.0).astype(BF16)], axis=1)

    def scores(qx, ks, tk, mask_fn):
        s = lax.dot_general(kx_ref[ks:ks + tk, :], qx, (((1,), (1,)), ((), ())),
                            preferred_element_type=F32)
        if mask_fn is not None:
            key = lax.broadcasted_iota(jnp.int32, s.shape, 0)
            qry = lax.broadcasted_iota(jnp.int32, s.shape, 1)
            s = jnp.where(mask_fn(key, qry), s, NEG)
        return s

    def partial_softmax(s, ks, tk):
        m = jnp.max(s, axis=0, keepdims=True)
        p = jnp.exp2(s - m).astype(BF16)
        acc = jnp.dot(vt_ref[:, ks:ks + tk], p, preferred_element_type=F32)
        return m, acc

    def finish(parts, q0, n):
        m = functools.reduce(jnp.maximum, [pm for pm, _ in parts])
        acc = 0.0
        for pm, pacc in parts:
            acc = acc + jnp.exp2(pm - m) * pacc
        fz = fz_ref[0, q0:q0 + n, :].astype(F32)
        o = (acc[0:FOX_DH] / acc[FOX_DH:FOX_DH + 1]).T * _silu(fz)
        o_ref[0, q0:q0 + n, :] = o.astype(o_ref.dtype)

    work = [(0, PAD, 0, PAD, lambda key, qry: (key >= N_ZERO) & (key <= qry))]
    for i in range(nq):
        q0 = PAD + i * tq
        work.append((q0, tq, 0, PAD, lambda key, qry: key >= N_ZERO))
        work += [(q0, tq, PAD + j * tq, tq, None) for j in range(i)]
        work.append((q0, tq, q0, tq, lambda key, qry: key <= qry))

    qx_of = {}

    def issue(item):
        q0, n, ks, tk, mask_fn = item
        if q0 not in qx_of:
            qx_of.clear()
            qx_of[q0] = extend_q(q_ref[0, q0:q0 + n, :])
        return scores(qx_of[q0], ks, tk, mask_fn)

    parts = []
    s_next = issue(work[0])
    for idx, (q0, n, ks, tk, _) in enumerate(work):
        s_cur = s_next
        if idx + 1 < len(work):
            s_next = issue(work[idx + 1])
        parts.append(partial_softmax(s_cur, ks, tk))
        if idx + 1 == len(work) or work[idx + 1][0] != q0:
            finish(parts, q0, n)
            parts = []


def _fox(z3, gcol):
    b, lp, _ = z3.shape
    n_real = lp - PAD
    tq = _largest_divisor(n_real, (512, 256, 128))
    blk = lambda off: pl.BlockSpec((1, lp, FOX_DH),
                                   lambda i, h, off=off: (i, 0, off // FOX_DH + h))
    return pl.pallas_call(
        functools.partial(_fox_kernel, tq=tq, nq=n_real // tq),
        out_shape=jax.ShapeDtypeStruct((b, lp, FOX_W), BF16),
        grid=(b, FOX_HEADS),
        in_specs=[blk(Z_FQ), blk(Z_FK), blk(Z_FV), blk(Z_FZ),
                  pl.BlockSpec((1, lp, 128), lambda i, h: (i, 0, 0))],
        out_specs=pl.BlockSpec((1, lp, FOX_DH), lambda i, h: (i, 0, h)),
        scratch_shapes=[pltpu.VMEM((lp, 2 * FOX_DH), BF16),
                        pltpu.VMEM((FOX_DH + 16, lp), BF16)],
        compiler_params=pltpu.CompilerParams(
            dimension_semantics=("parallel", "arbitrary"),
            vmem_limit_bytes=VMEM_LIMIT),
        name="fox",
    )(z3, z3, z3, z3, gcol)


def _mlstm_kernel(mq_ref, mk_ref, mv_ref, mo_ref, mz_ref, gcol_ref, grow_ref,
                  cw_ref, cb_ref, gh_ref, o_ref,
                  xbuf_ref, c_ref, n_ref, m_ref):
    c = pl.program_id(1)

    @pl.when(c == 0)
    def _():
        xbuf_ref[0:8, :] = jnp.zeros((8, 2 * ML_HEADS * ML_DK), F32)
        c_ref[...] = jnp.zeros_like(c_ref)
        n_ref[...] = jnp.zeros_like(n_ref)
        m_ref[...] = jnp.zeros_like(m_ref)

    half = ML_HEADS * ML_DK
    xbuf_ref[8:8 + CHUNK, 0:half] = mq_ref[0].astype(F32)
    xbuf_ref[8:8 + CHUNK, half:2 * half] = mk_ref[0].astype(F32)
    y = cb_ref[...]
    for j in range(CONV_W):
        y = y + cw_ref[j:j + 1, :] * xbuf_ref[pl.ds(8 - (CONV_W - 1) + j, CHUNK), :]
    xbuf_ref[0:8, :] = xbuf_ref[CHUNK:CHUNK + 8, :]
    qk = _silu(y)

    gcol = gcol_ref[0]
    grow = grow_ref[0]
    rows = lax.broadcasted_iota(jnp.int32, (CHUNK, CHUNK), 0)
    cols = lax.broadcasted_iota(jnp.int32, (CHUNK, CHUNK), 1)
    causal = cols <= rows

    for hd in range(ML_HEADS):
        q = (qk[:, hd * ML_DK:(hd + 1) * ML_DK] * (ML_DK ** -0.5)).astype(BF16)
        k = qk[:, half + hd * ML_DK: half + (hd + 1) * ML_DK]
        v = mv_ref[0, :, hd * ML_DV:(hd + 1) * ML_DV]
        b_col = gcol[:, G_LF + hd:G_LF + hd + 1]
        li_col = gcol[:, G_LI + hd:G_LI + hd + 1]
        b_row = grow[G_LF + hd:G_LF + hd + 1, :]
        li_row = grow[G_LI + hd:G_LI + hd + 1, :]
        m_prev = m_ref[hd:hd + 1, 0:1]
        c_prev = c_ref[hd]
        n_prev = n_ref[hd:hd + 1, :]

        log_d = jnp.where(causal, b_col - b_row + li_row, NEG)
        log_inter = b_col + m_prev
        m_t = jnp.maximum(log_inter, jnp.max(log_d, axis=-1, keepdims=True))
        d = jnp.exp(log_d - m_t)
        inter = jnp.exp(log_inter - m_t)
        s = lax.dot_general(q, k.astype(BF16), (((1,), (1,)), ((), ())),
                            preferred_element_type=F32) * d
        num = (inter * jnp.dot(q, c_prev.astype(BF16), preferred_element_type=F32)
               + jnp.dot(s.astype(BF16), v, preferred_element_type=F32))
        den = (inter * jnp.sum(q.astype(F32) * n_prev, axis=-1, keepdims=True)
               + jnp.sum(s, axis=-1, keepdims=True))
        hh = num / jnp.maximum(jnp.abs(den), jnp.exp(-m_t))

        b_last = b_col[CHUNK - 1:CHUNK, :]
        log_w = b_last - b_col + li_col
        m_new = jnp.maximum(b_last + m_prev, jnp.max(log_w, axis=0, keepdims=True))
        w = jnp.exp(log_w - m_new)
        decay = jnp.exp(b_last + m_prev - m_new)
        kw = k * w
        c_ref[hd] = decay * c_prev + jnp.dot(kw.T.astype(BF16), v, preferred_element_type=F32)
        n_ref[hd:hd + 1, :] = decay * n_prev + jnp.sum(kw, axis=0, keepdims=True)
        m_ref[hd:hd + 1, :] = jnp.broadcast_to(m_new, (1, 128))

        mo = mo_ref[0, :, hd * ML_DV:(hd + 1) * ML_DV].astype(F32)
        mz = mz_ref[0, :, hd * ML_DV:(hd + 1) * ML_DV].astype(F32)
        hb = _sigmoid(mo) * hh
        ms = jnp.mean(hb * hb, axis=-1, keepdims=True)
        hb = hb * lax.rsqrt(ms + EPS) * gh_ref[:, hd * ML_DV:(hd + 1) * ML_DV]
        o_ref[0, :, hd * ML_DV:(hd + 1) * ML_DV] = (hb * _silu(mz)).astype(o_ref.dtype)


def _mlstm(z3, gcol, grow, conv_w, conv_b, g_head):
    b, lp, _ = z3.shape
    half = ML_HEADS * ML_DK
    zblk = lambda width, off: pl.BlockSpec(
        (1, CHUNK, width), lambda i, c, off=off, width=width: (i, c, off // width))
    full = lambda shape: pl.BlockSpec(shape, lambda i, c: (0,) * len(shape))
    return pl.pallas_call(
        _mlstm_kernel,
        out_shape=jax.ShapeDtypeStruct((b, lp, ML_W), BF16),
        grid=(b, lp // CHUNK),
        in_specs=[zblk(half, Z_MQ), zblk(half, Z_MK), zblk(ML_W, Z_MV),
                  zblk(ML_W, Z_MO), zblk(ML_W, Z_MZ),
                  pl.BlockSpec((1, CHUNK, 128), lambda i, c: (i, c, 0)),
                  pl.BlockSpec((1, 16, CHUNK), lambda i, c: (i, 0, c)),
                  full((CONV_W, 2 * half)), full((1, 2 * half)), full((1, ML_W))],
        out_specs=pl.BlockSpec((1, CHUNK, ML_W), lambda i, c: (i, c, 0)),
        scratch_shapes=[pltpu.VMEM((8 + CHUNK, 2 * half), F32),
                        pltpu.VMEM((ML_HEADS, ML_DK, ML_DV), F32),
                        pltpu.VMEM((ML_HEADS, ML_DK), F32),
                        pltpu.VMEM((ML_HEADS, 128), F32)],
        compiler_params=pltpu.CompilerParams(
            dimension_semantics=("parallel", "arbitrary"),
            vmem_limit_bytes=VMEM_LIMIT),
        name="mlstm",
    )(z3, z3, z3, z3, z3, gcol, grow, conv_w, conv_b, g_head)


def _outproj_kernel(oa_ref, hb_ref, ga_ref, gb_ref, x_ref, wa_ref, wb_ref, wo_ref,
                    g_ref, o_ref):
    ya = jnp.dot(oa_ref[...], wa_ref[...], preferred_element_type=F32)
    yb = jnp.dot(hb_ref[...], wb_ref[...], preferred_element_type=F32)
    merged = (_sigmoid(ga_ref[...].astype(F32)) * ya
              + _sigmoid(gb_ref[...].astype(F32)) * yb)
    y = jnp.dot(merged.astype(BF16), wo_ref[...], preferred_element_type=F32)
    ms = jnp.mean(y * y, axis=-1, keepdims=True)
    o_ref[...] = x_ref[...] + y * lax.rsqrt(ms + EPS) * g_ref[...]


def _outproj(oa2, hb2, z2, x2, wa, wb, wo, g):
    t = x2.shape[0]
    tm = _largest_divisor(t, (512, 256, 128))
    rows = lambda width, col=0: pl.BlockSpec((tm, width), lambda i, col=col: (i, col))
    full = lambda shape: pl.BlockSpec(shape, lambda i: (0,) * len(shape))
    return pl.pallas_call(
        _outproj_kernel,
        out_shape=jax.ShapeDtypeStruct((t, D_MODEL), F32),
        grid=(t // tm,),
        in_specs=[rows(FOX_W), rows(ML_W),
                  rows(D_MODEL, Z_GA // D_MODEL), rows(D_MODEL, Z_GB // D_MODEL),
                  rows(D_MODEL),
                  full((FOX_W, D_MODEL)), full((ML_W, D_MODEL)), full((D_MODEL, D_MODEL)),
                  full((1, D_MODEL))],
        out_specs=rows(D_MODEL),
        compiler_params=pltpu.CompilerParams(
            dimension_semantics=("parallel",),
            vmem_limit_bytes=VMEM_LIMIT),
        name="outproj",
    )(oa2, hb2, z2, z2, x2, wa, wb, wo, g)


def _layer(h, g_pre, g_post, wz, wg, bias, conv_w, conv_b, g_head, w_a, w_b, w_o):
    b, lp, d = h.shape
    x2 = h.reshape(b * lp, d)
    z2, zg2 = _inproj(x2, g_pre[None, :], wz, wg)
    z3 = z2.reshape(b, lp, NZ)
    gcol, grow = _gates(zg2.reshape(b, lp, 128), bias[None, :])
    oa = _fox(z3, gcol)
    hb = _mlstm(z3, gcol, grow, conv_w, conv_b[None, :], g_head[None, :])
    out = _outproj(oa.reshape(b * lp, FOX_W), hb.reshape(b * lp, ML_W), z2, x2,
                   w_a, w_b, w_o, g_post[None, :])
    return out.reshape(b, lp, d)


def kernel(x, meta_tokens, norm_pre, norm_post, w_in, b_fox_f, conv_w, conv_b,
           b_mlstm_i, b_mlstm_f, mlstm_head_norm, w_a, w_b, w_o):
    b, seq, d = x.shape
    depth = norm_pre.shape[0]
    assert d == D_MODEL and seq % CHUNK == 0
    assert meta_tokens.shape == (N_META, D_MODEL)

    (fq, fk, fv, ff, fz, mqk, mv, mi, mf, mo, mz, ga, gb) = jnp.split(w_in, SPLIT_POINTS, axis=2)
    wz = jnp.concatenate([fq * (LOG2E * FOX_DH ** -0.5), fk, fv, fz, mqk, mv, mo, mz, ga, gb],
                         axis=2).astype(BF16)
    wg = jnp.concatenate([ff, mi, mf, jnp.zeros((depth, d, 128 - 16), F32)], axis=2).astype(BF16)
    bias = jnp.concatenate([b_fox_f, b_mlstm_i, b_mlstm_f, jnp.zeros((depth, 128 - 16), F32)], axis=1)
    w_a, w_b, w_o = w_a.astype(BF16), w_b.astype(BF16), w_o.astype(BF16)

    lead = jnp.concatenate([jnp.zeros((N_ZERO, d), x.dtype), meta_tokens.astype(x.dtype)], axis=0)
    h = jnp.concatenate([jnp.broadcast_to(lead[None], (b, PAD, d)), x], axis=1)
    for l in range(depth):
        h = _layer(h, norm_pre[l], norm_post[l], wz[l], wg[l], bias[l], conv_w[l], conv_b[l],
                   mlstm_head_norm[l], w_a[l], w_b[l], w_o[l])
    return h[:, PAD:]
```

```python
import functools

import jax
import jax.numpy as jnp
import numpy as np
from jax import lax
from jax.experimental import pallas as pl
from jax.experimental.pallas import tpu as pltpu

F32 = jnp.float32
BF16 = jnp.bfloat16

D_MODEL = 1024
N_META = 16
PAD = 128
N_ZERO = PAD - N_META
FOX_HEADS = 8
FOX_DH = 128
FOX_W = FOX_HEADS * FOX_DH
ML_HEADS = 4
ML_DK = 256
ML_DV = 512
ML_W = ML_HEADS * ML_DV
CHUNK = 128
CONV_W = 4
EPS = 1e-6
NEG = -1e30
LOG2E = 1.4426950408889634

SPLIT_SIZES = (FOX_W, FOX_W, FOX_W, FOX_HEADS, FOX_W, 2 * ML_HEADS * ML_DK, ML_W,
               ML_HEADS, ML_HEADS, ML_W, ML_W, D_MODEL, D_MODEL)
SPLIT_POINTS = tuple(int(p) for p in np.cumsum(SPLIT_SIZES)[:-1])
N_IN = sum(SPLIT_SIZES)

Z_FQ, Z_FK, Z_FV, Z_FZ = 0, 1024, 2048, 3072
Z_MQ, Z_MK, Z_MV, Z_MO, Z_MZ = 4096, 5120, 6144, 8192, 10240
Z_GA, Z_GB = 12288, 13312
NZ = 14336
G_FOX, G_LI, G_LF = 0, 8, 12

VMEM_LIMIT = 48 * 1024 * 1024


def _sigmoid(x):
    return 0.5 * jnp.tanh(0.5 * x) + 0.5


def _silu(x):
    h = 0.5 * x
    return h * jnp.tanh(h) + h


def _largest_divisor(n, candidates):
    for c in candidates:
        if n % c == 0:
            return c
    raise ValueError(f"no tile in {candidates} divides {n}")


def _wprep_kernel(w_ref, wz_ref, wg_ref):
    x = w_ref[0]
    p_ff, p_fz, p_mi, p_mo = SPLIT_POINTS[2], SPLIT_POINTS[3], SPLIT_POINTS[6], SPLIT_POINTS[8]
    wz_ref[0, :, Z_FQ:Z_FK] = (x[:, 0:FOX_W] * (LOG2E * FOX_DH ** -0.5)).astype(BF16)
    wz_ref[0, :, Z_FK:Z_FZ] = x[:, FOX_W:p_ff].astype(BF16)
    wz_ref[0, :, Z_FZ:Z_MO] = x[:, p_fz:p_mi].astype(BF16)
    wz_ref[0, :, Z_MO:NZ] = x[:, p_mo:N_IN].astype(BF16)
    gates = jnp.concatenate([x[:, p_ff:p_fz], x[:, p_mi:p_mo],
                             jnp.zeros((x.shape[0], 128 - 16), F32)], axis=1)
    wg_ref[0] = gates.astype(BF16)


def _wprep(w_in):
    depth, d, n_in = w_in.shape
    tr = 128
    return pl.pallas_call(
        _wprep_kernel,
        out_shape=(jax.ShapeDtypeStruct((depth, d, NZ), BF16),
                   jax.ShapeDtypeStruct((depth, d, 128), BF16)),
        grid=(depth, d // tr),
        in_specs=[pl.BlockSpec((1, tr, n_in), lambda l, i: (l, i, 0))],
        out_specs=(pl.BlockSpec((1, tr, NZ), lambda l, i: (l, i, 0)),
                   pl.BlockSpec((1, tr, 128), lambda l, i: (l, i, 0))),
        compiler_params=pltpu.CompilerParams(
            dimension_semantics=("parallel", "parallel"),
            vmem_limit_bytes=VMEM_LIMIT),
        name="wprep",
    )(w_in)


def _inproj_kernel(x_ref, g_ref, w_ref, wg_ref, z_ref, zg_ref, xn_ref, *, n_sub):
    @pl.when(pl.program_id(1) == 0)
    def _():
        x = x_ref[...]
        ms = jnp.mean(x * x, axis=-1, keepdims=True)
        xn = (x * lax.rsqrt(ms + EPS) * g_ref[...]).astype(BF16)
        xn_ref[...] = xn
        zg_ref[...] = jnp.dot(xn, wg_ref[...], preferred_element_type=F32)

    xn = xn_ref[...]
    ts = w_ref.shape[1] // n_sub
    for n in range(n_sub):
        z_ref[:, n * ts:(n + 1) * ts] = jnp.dot(
            xn, w_ref[:, n * ts:(n + 1) * ts], preferred_element_type=F32).astype(z_ref.dtype)


def _inproj(x2, g, wz, wg):
    t = x2.shape[0]
    tm = _largest_divisor(t, (1024, 512, 256, 128))
    tn, n_sub = 2048, 4
    return pl.pallas_call(
        functools.partial(_inproj_kernel, n_sub=n_sub),
        out_shape=(jax.ShapeDtypeStruct((t, NZ), BF16),
                   jax.ShapeDtypeStruct((t, 128), F32)),
        grid=(t // tm, NZ // tn),
        in_specs=[pl.BlockSpec((tm, D_MODEL), lambda i, j: (i, 0)),
                  pl.BlockSpec((1, D_MODEL), lambda i, j: (0, 0)),
                  pl.BlockSpec((D_MODEL, tn), lambda i, j: (0, j)),
                  pl.BlockSpec((D_MODEL, 128), lambda i, j: (0, 0))],
        out_specs=(pl.BlockSpec((tm, tn), lambda i, j: (i, j)),
                   pl.BlockSpec((tm, 128), lambda i, j: (i, 0))),
        scratch_shapes=[pltpu.VMEM((tm, D_MODEL), BF16)],
        compiler_params=pltpu.CompilerParams(
            dimension_semantics=("parallel", "arbitrary"),
            vmem_limit_bytes=VMEM_LIMIT),
        name="inproj",
    )(x2, g, wz, wg)


def _gates_kernel(zg_ref, bias_ref, gcol_ref, grow_ref):
    row = lax.broadcasted_iota(jnp.int32, (CHUNK, 128), 0)
    lane = lax.broadcasted_iota(jnp.int32, (CHUNK, 128), 1)
    is_fox = lane < G_LI
    is_li = (lane >= G_LI) & (lane < G_LF)
    is_lf = (lane >= G_LF) & (lane < G_LF + ML_HEADS)
    tri = (row >= lane).astype(BF16)
    carry = jnp.zeros((1, 128), F32)
    for c in range(zg_ref.shape[1] // CHUNK):
        r0 = c * CHUNK
        a = zg_ref[0, r0:r0 + CHUNK, :] + bias_ref[...]
        log_sig = jnp.minimum(a, 0.0) - jnp.log1p(jnp.exp(-jnp.abs(a)))
        real = (row >= N_ZERO) if c == 0 else None
        v = jnp.where(is_fox | is_lf, log_sig, 0.0)
        if real is not None:
            v = jnp.where(real, v, 0.0)

        hi = v.astype(BF16)
        r1 = v - hi.astype(F32)
        mid = r1.astype(BF16)
        lo = (r1 - mid.astype(F32)).astype(BF16)
        cum = (jnp.dot(tri, hi, preferred_element_type=F32)
               + jnp.dot(tri, mid, preferred_element_type=F32)
               + jnp.dot(tri, lo, preferred_element_type=F32))
        cum = cum + carry
        carry = jnp.where(lane[:1] < G_LI, cum[CHUNK - 1:CHUNK, :], 0.0)

        log_i = a if real is None else jnp.where(real, a, NEG)
        out = jnp.where(is_li, log_i, cum)
        gcol_ref[0, r0:r0 + CHUNK, :] = out
        grow_ref[0, :, r0:r0 + CHUNK] = out.T[:16, :]


def _gates(zg3, bias):
    b, lp, _ = zg3.shape
    return pl.pallas_call(
        _gates_kernel,
        out_shape=(jax.ShapeDtypeStruct((b, lp, 128), F32),
                   jax.ShapeDtypeStruct((b, 16, lp), F32)),
        grid=(b,),
        in_specs=[pl.BlockSpec((1, lp, 128), lambda i: (i, 0, 0)),
                  pl.BlockSpec((1, 128), lambda i: (0, 0))],
        out_specs=(pl.BlockSpec((1, lp, 128), lambda i: (i, 0, 0)),
                   pl.BlockSpec((1, 16, lp), lambda i: (i, 0, 0))),
        compiler_params=pltpu.CompilerParams(dimension_semantics=("parallel",)),
        name="gates",
    )(zg3, bias)


def _fox_kernel(q_ref, k_ref, v_ref, fz_ref, gcol_ref, o_ref, kx_ref, vt_ref, *, tq, nq):
    h = pl.program_id(1)
    lp = k_ref.shape[1]
    r128 = lax.broadcasted_iota(jnp.int32, (128, 128), 0)
    c128 = lax.broadcasted_iota(jnp.int32, (128, 128), 1)
    sel = [((r128 == G_FOX + h) & (c128 == j)).astype(BF16) for j in range(3)]

    for c in range(lp // 128):
        r0 = c * 128
        kx_ref[r0:r0 + 128, 0:FOX_DH] = k_ref[0, r0:r0 + 128, :]
        g = gcol_ref[0, r0:r0 + 128, :] * (-LOG2E)
        hi = g.astype(BF16)
        r1 = g - hi.astype(F32)
        mid = r1.astype(BF16)
        lo = (r1 - mid.astype(F32)).astype(BF16)
        aug = (jnp.dot(hi, sel[0], preferred_element_type=F32)
               + jnp.dot(mid, sel[1], preferred_element_type=F32)
               + jnp.dot(lo, sel[2], preferred_element_type=F32))
        kx_ref[r0:r0 + 128, FOX_DH:2 * FOX_DH] = aug.astype(BF16)
        vt_ref[0:FOX_DH, r0:r0 + 128] = v_ref[0, r0:r0 + 128, :].astype(F32).T.astype(BF16)
    ones_row = lax.broadcasted_iota(jnp.int32, (16, lp), 0) == 0
    vt_ref[FOX_DH:FOX_DH + 16, :] = jnp.where(ones_row, 1.0, 0.0).astype(BF16)

    def extend_q(q):
        lane = lax.broadcasted_iota(jnp.int32, q.shape, 1)
        return jnp.concatenate([q, jnp.where(lane < 3, 1.0, 0.0).astype(BF16)], axis=1)

    def scores(qx, ks, tk, mask_fn):
        s = lax.dot_general(kx_ref[ks:ks + tk, :], qx, (((1,), (1,)), ((), ())),
                            preferred_element_type=F32)
        if mask_fn is not None:
            key = lax.broadcasted_iota(jnp.int32, s.shape, 0)
            qry = lax.broadcasted_iota(jnp.int32, s.shape, 1)
            s = jnp.where(mask_fn(key, qry), s, NEG)
        return s

    def partial_softmax(s, ks, tk):
        m = jnp.max(s, axis=0, keepdims=True)
        p = jnp.exp2(s - m).astype(BF16)
        acc = jnp.dot(vt_ref[:, ks:ks + tk], p, preferred_element_type=F32)
        return m, acc

    def finish(parts, q0, n):
        m = functools.reduce(jnp.maximum, [pm for pm, _ in parts])
        acc = 0.0
        for pm, pacc in parts:
            acc = acc + jnp.exp2(pm - m) * pacc
        fz = fz_ref[0, q0:q0 + n, :].astype(F32)
        o = (acc[0:FOX_DH] / acc[FOX_DH:FOX_DH + 1]).T * _silu(fz)
        o_ref[0, q0:q0 + n, :] = o.astype(o_ref.dtype)

    work = [(0, PAD, 0, PAD, lambda key, qry: (key >= N_ZERO) & (key <= qry))]
    for i in range(nq):
        q0 = PAD + i * tq
        work.append((q0, tq, 0, PAD, lambda key, qry: key >= N_ZERO))
        work += [(q0, tq, PAD + j * tq, tq, None) for j in range(i)]
        work.append((q0, tq, q0, tq, lambda key, qry: key <= qry))

    qx_of = {}

    def issue(item):
        q0, n, ks, tk, mask_fn = item
        if q0 not in qx_of:
            qx_of.clear()
            qx_of[q0] = extend_q(q_ref[0, q0:q0 + n, :])
        return scores(qx_of[q0], ks, tk, mask_fn)

    parts = []
    s_next = issue(work[0])
    for idx, (q0, n, ks, tk, _) in enumerate(work):
        s_cur = s_next
        if idx + 1 < len(work):
            s_next = issue(work[idx + 1])
        parts.append(partial_softmax(s_cur, ks, tk))
        if idx + 1 == len(work) or work[idx + 1][0] != q0:
            finish(parts, q0, n)
            parts = []


def _fox(z3, gcol):
    b, lp, _ = z3.shape
    n_real = lp - PAD
    tq = _largest_divisor(n_real, (512, 256, 128))
    blk = lambda off: pl.BlockSpec((1, lp, FOX_DH),
                                   lambda i, h, off=off: (i, 0, off // FOX_DH + h))
    return pl.pallas_call(
        functools.partial(_fox_kernel, tq=tq, nq=n_real // tq),
        out_shape=jax.ShapeDtypeStruct((b, lp, FOX_W), BF16),
        grid=(b, FOX_HEADS),
        in_specs=[blk(Z_FQ), blk(Z_FK), blk(Z_FV), blk(Z_FZ),
                  pl.BlockSpec((1, lp, 128), lambda i, h: (i, 0, 0))],
        out_specs=pl.BlockSpec((1, lp, FOX_DH), lambda i, h: (i, 0, h)),
        scratch_shapes=[pltpu.VMEM((lp, 2 * FOX_DH), BF16),
                        pltpu.VMEM((FOX_DH + 16, lp), BF16)],
        compiler_params=pltpu.CompilerParams(
            dimension_semantics=("parallel", "arbitrary"),
            vmem_limit_bytes=VMEM_LIMIT),
        name="fox",
    )(z3, z3, z3, z3, gcol)


def _mlstm_kernel(mq_ref, mk_ref, mv_ref, mo_ref, mz_ref, gcol_ref, grow_ref,
                  cw_ref, cb_ref, gh_ref, o_ref,
                  xprev_ref, c_ref, n_ref, m_ref):
    c = pl.program_id(1)

    @pl.when(c == 0)
    def _():
        xprev_ref[...] = jnp.zeros_like(xprev_ref)
        c_ref[...] = jnp.zeros_like(c_ref)
        n_ref[...] = jnp.zeros_like(n_ref)
        m_ref[...] = jnp.zeros_like(m_ref)

    half = ML_HEADS * ML_DK
    x_cur = jnp.concatenate([mq_ref[0], mk_ref[0]], axis=1)
    x_cat = jnp.concatenate([x_cur, xprev_ref[...]], axis=0)
    xprev_ref[...] = x_cur
    r = lax.broadcasted_iota(jnp.int32, ((CONV_W - 1) * CHUNK, 2 * CHUNK), 0)
    col = lax.broadcasted_iota(jnp.int32, ((CONV_W - 1) * CHUNK, 2 * CHUNK), 1)
    t = r % CHUNK
    src = t - (r // CHUNK + 1)
    shift = jnp.where(col == jnp.where(src < 0, src + 2 * CHUNK, src), 1.0, 0.0).astype(BF16)
    shifted = jnp.dot(shift, x_cat, preferred_element_type=F32)
    y = cb_ref[...] + cw_ref[CONV_W - 1:CONV_W, :] * x_cur.astype(F32)
    for j in range(1, CONV_W):
        y = y + cw_ref[CONV_W - 1 - j:CONV_W - j, :] * shifted[(j - 1) * CHUNK:j * CHUNK, :]
    qk = _silu(y)

    gcol = gcol_ref[0]
    grow = grow_ref[0]
    rows = lax.broadcasted_iota(jnp.int32, (CHUNK, CHUNK), 0)
    cols = lax.broadcasted_iota(jnp.int32, (CHUNK, CHUNK), 1)
    causal = cols <= rows

    for hd in range(ML_HEADS):
        q = (qk[:, hd * ML_DK:(hd + 1) * ML_DK] * (ML_DK ** -0.5)).astype(BF16)
        k = qk[:, half + hd * ML_DK: half + (hd + 1) * ML_DK]
        v = mv_ref[0, :, hd * ML_DV:(hd + 1) * ML_DV]
        b_col = gcol[:, G_LF + hd:G_LF + hd + 1]
        li_col = gcol[:, G_LI + hd:G_LI + hd + 1]
        b_row = grow[G_LF + hd:G_LF + hd + 1, :]
        li_row = grow[G_LI + hd:G_LI + hd + 1, :]
        m_prev = m_ref[hd:hd + 1, 0:1]
        c_prev = c_ref[hd]
        n_prev = n_ref[hd:hd + 1, :]

        log_d = jnp.where(causal, b_col - b_row + li_row, NEG)
        log_inter = b_col + m_prev
        m_t = jnp.maximum(log_inter, jnp.max(log_d, axis=-1, keepdims=True))
        d = jnp.exp(log_d - m_t)
        inter = jnp.exp(log_inter - m_t)
        s = lax.dot_general(q, k.astype(BF16), (((1,), (1,)), ((), ())),
                            preferred_element_type=F32) * d
        num = (inter * jnp.dot(q, c_prev.astype(BF16), preferred_element_type=F32)
               + jnp.dot(s.astype(BF16), v, preferred_element_type=F32))
        den = (inter * jnp.sum(q.astype(F32) * n_prev, axis=-1, keepdims=True)
               + jnp.sum(s, axis=-1, keepdims=True))
        hh = num / jnp.maximum(jnp.abs(den), jnp.exp(-m_t))

        b_last = b_col[CHUNK - 1:CHUNK, :]
        log_w = b_last - b_col + li_col
        m_new = jnp.maximum(b_last + m_prev, jnp.max(log_w, axis=0, keepdims=True))
        w = jnp.exp(log_w - m_new)
        decay = jnp.exp(b_last + m_prev - m_new)
        kw = k * w
        c_ref[hd] = decay * c_prev + jnp.dot(kw.T.astype(BF16), v, preferred_element_type=F32)
        n_ref[hd:hd + 1, :] = decay * n_prev + jnp.sum(kw, axis=0, keepdims=True)
        m_ref[hd:hd + 1, :] = jnp.broadcast_to(m_new, (1, 128))

        mo = mo_ref[0, :, hd * ML_DV:(hd + 1) * ML_DV].astype(F32)
        mz = mz_ref[0, :, hd * ML_DV:(hd + 1) * ML_DV].astype(F32)
        hb = _sigmoid(mo) * hh
        ms = jnp.mean(hb * hb, axis=-1, keepdims=True)
        hb = hb * lax.rsqrt(ms + EPS) * gh_ref[:, hd * ML_DV:(hd + 1) * ML_DV]
        o_ref[0, :, hd * ML_DV:(hd + 1) * ML_DV] = (hb * _silu(mz)).astype(o_ref.dtype)


def _mlstm(z3, gcol, grow, conv_w, conv_b, g_head):
    b, lp, _ = z3.shape
    half = ML_HEADS * ML_DK
    zblk = lambda width, off: pl.BlockSpec(
        (1, CHUNK, width), lambda i, c, off=off, width=width: (i, c, off // width))
    full = lambda shape: pl.BlockSpec(shape, lambda i, c: (0,) * len(shape))
    return pl.pallas_call(
        _mlstm_kernel,
        out_shape=jax.ShapeDtypeStruct((b, lp, ML_W), BF16),
        grid=(b, lp // CHUNK),
        in_specs=[zblk(half, Z_MQ), zblk(half, Z_MK), zblk(ML_W, Z_MV),
                  zblk(ML_W, Z_MO), zblk(ML_W, Z_MZ),
                  pl.BlockSpec((1, CHUNK, 128), lambda i, c: (i, c, 0)),
                  pl.BlockSpec((1, 16, CHUNK), lambda i, c: (i, 0, c)),
                  full((CONV_W, 2 * half)), full((1, 2 * half)), full((1, ML_W))],
        out_specs=pl.BlockSpec((1, CHUNK, ML_W), lambda i, c: (i, c, 0)),
        scratch_shapes=[pltpu.VMEM((CHUNK, 2 * half), BF16),
                        pltpu.VMEM((ML_HEADS, ML_DK, ML_DV), F32),
                        pltpu.VMEM((ML_HEADS, ML_DK), F32),
                        pltpu.VMEM((ML_HEADS, 128), F32)],
        compiler_params=pltpu.CompilerParams(
            dimension_semantics=("parallel", "arbitrary"),
            vmem_limit_bytes=VMEM_LIMIT),
        name="mlstm",
    )(z3, z3, z3, z3, z3, gcol, grow, conv_w, conv_b, g_head)


def _outproj_kernel(oa_ref, hb_ref, ga_ref, gb_ref, x_ref, wa_ref, wb_ref, wo_ref,
                    g_ref, o_ref):
    ya = jnp.dot(oa_ref[...], wa_ref[...], preferred_element_type=F32)
    yb = jnp.dot(hb_ref[...], wb_ref[...], preferred_element_type=F32)
    merged = (_sigmoid(ga_ref[...].astype(F32)) * ya
              + _sigmoid(gb_ref[...].astype(F32)) * yb)
    y = jnp.dot(merged.astype(BF16), wo_ref[...], preferred_element_type=F32)
    ms = jnp.mean(y * y, axis=-1, keepdims=True)
    o_ref[...] = x_ref[...] + y * lax.rsqrt(ms + EPS) * g_ref[...]


def _outproj(oa2, hb2, z2, x2, wa, wb, wo, g):
    t = x2.shape[0]
    tm = _largest_divisor(t, (512, 256, 128))
    rows = lambda width, col=0: pl.BlockSpec((tm, width), lambda i, col=col: (i, col))
    full = lambda shape: pl.BlockSpec(shape, lambda i: (0,) * len(shape))
    return pl.pallas_call(
        _outproj_kernel,
        out_shape=jax.ShapeDtypeStruct((t, D_MODEL), F32),
        grid=(t // tm,),
        in_specs=[rows(FOX_W), rows(ML_W),
                  rows(D_MODEL, Z_GA // D_MODEL), rows(D_MODEL, Z_GB // D_MODEL),
                  rows(D_MODEL),
                  full((FOX_W, D_MODEL)), full((ML_W, D_MODEL)), full((D_MODEL, D_MODEL)),
                  full((1, D_MODEL))],
        out_specs=rows(D_MODEL),
        compiler_params=pltpu.CompilerParams(
            dimension_semantics=("parallel",),
            vmem_limit_bytes=VMEM_LIMIT),
        name="outproj",
    )(oa2, hb2, z2, z2, x2, wa, wb, wo, g)


def _layer(h, g_pre, g_post, wz, wg, bias, conv_w, conv_b, g_head, w_a, w_b, w_o):
    b, lp, d = h.shape
    x2 = h.reshape(b * lp, d)
    z2, zg2 = _inproj(x2, g_pre[None, :], wz, wg)
    z3 = z2.reshape(b, lp, NZ)
    gcol, grow = _gates(zg2.reshape(b, lp, 128), bias[None, :])
    oa = _fox(z3, gcol)
    hb = _mlstm(z3, gcol, grow, conv_w, conv_b[None, :], g_head[None, :])
    out = _outproj(oa.reshape(b * lp, FOX_W), hb.reshape(b * lp, ML_W), z2, x2,
                   w_a, w_b, w_o, g_post[None, :])
    return out.reshape(b, lp, d)


def kernel(x, meta_tokens, norm_pre, norm_post, w_in, b_fox_f, conv_w, conv_b,
           b_mlstm_i, b_mlstm_f, mlstm_head_norm, w_a, w_b, w_o):
    b, seq, d = x.shape
    depth = norm_pre.shape[0]
    assert d == D_MODEL and seq % CHUNK == 0
    assert meta_tokens.shape == (N_META, D_MODEL)

    assert w_in.shape[1:] == (D_MODEL, N_IN)
    wz, wg = _wprep(w_in)
    bias = jnp.concatenate([b_fox_f, b_mlstm_i, b_mlstm_f, jnp.zeros((depth, 128 - 16), F32)], axis=1)
    w_a, w_b, w_o = w_a.astype(BF16), w_b.astype(BF16), w_o.astype(BF16)

    lead = jnp.concatenate([jnp.zeros((N_ZERO, d), x.dtype), meta_tokens.astype(x.dtype)], axis=0)
    h = jnp.concatenate([jnp.broadcast_to(lead[None], (b, PAD, d)), x], axis=1)
    for l in range(depth):
        h = _layer(h, norm_pre[l], norm_post[l], wz[l], wg[l], bias[l], conv_w[l], conv_b[l],
                   mlstm_head_norm[l], w_a[l], w_b[l], w_o[l])
    return h[:, PAD:]
```

```python
import functools

import jax
import jax.numpy as jnp
import numpy as np
from jax import lax
from jax.experimental import pallas as pl
from jax.experimental.pallas import tpu as pltpu

F32 = jnp.float32
BF16 = jnp.bfloat16

D_MODEL = 1024
N_META = 16
PAD = 128
N_ZERO = PAD - N_META
FOX_HEADS = 8
FOX_DH = 128
FOX_W = FOX_HEADS * FOX_DH
ML_HEADS = 4
ML_DK = 256
ML_DV = 512
ML_W = ML_HEADS * ML_DV
CHUNK = 128
CONV_W = 4
EPS = 1e-6
NEG = -1e30
LOG2E = 1.4426950408889634

SPLIT_SIZES = (FOX_W, FOX_W, FOX_W, FOX_HEADS, FOX_W, 2 * ML_HEADS * ML_DK, ML_W,
               ML_HEADS, ML_HEADS, ML_W, ML_W, D_MODEL, D_MODEL)
SPLIT_POINTS = tuple(int(p) for p in np.cumsum(SPLIT_SIZES)[:-1])
N_IN = sum(SPLIT_SIZES)

Z_FQ, Z_FK, Z_FV, Z_FZ = 0, 1024, 2048, 3072
Z_MQ, Z_MK, Z_MV, Z_MO, Z_MZ = 4096, 5120, 6144, 8192, 10240
Z_GA, Z_GB = 12288, 13312
NZ = 14336
G_FOX, G_LI, G_LF = 0, 8, 12

VMEM_LIMIT = 48 * 1024 * 1024


def _sigmoid(x):
    return 0.5 * jnp.tanh(0.5 * x) + 0.5


def _silu(x):
    h = 0.5 * x
    return h * jnp.tanh(h) + h


def _largest_divisor(n, candidates):
    for c in candidates:
        if n % c == 0:
            return c
    raise ValueError(f"no tile in {candidates} divides {n}")


def _wprep_kernel(w_ref, wz_ref, wg_ref):
    x = w_ref[0]
    p_ff, p_fz, p_mi, p_mo = SPLIT_POINTS[2], SPLIT_POINTS[3], SPLIT_POINTS[6], SPLIT_POINTS[8]
    wz_ref[0, :, Z_FQ:Z_FK] = (x[:, 0:FOX_W] * (LOG2E * FOX_DH ** -0.5)).astype(BF16)
    wz_ref[0, :, Z_FK:Z_FZ] = x[:, FOX_W:p_ff].astype(BF16)
    wz_ref[0, :, Z_FZ:Z_MO] = x[:, p_fz:p_mi].astype(BF16)
    wz_ref[0, :, Z_MO:NZ] = x[:, p_mo:N_IN].astype(BF16)
    gates = jnp.concatenate([x[:, p_ff:p_fz], x[:, p_mi:p_mo],
                             jnp.zeros((x.shape[0], 128 - 16), F32)], axis=1)
    wg_ref[0] = gates.astype(BF16)


def _wprep(w_in):
    depth, d, n_in = w_in.shape
    tr = 128
    return pl.pallas_call(
        _wprep_kernel,
        out_shape=(jax.ShapeDtypeStruct((depth, d, NZ), BF16),
                   jax.ShapeDtypeStruct((depth, d, 128), BF16)),
        grid=(depth, d // tr),
        in_specs=[pl.BlockSpec((1, tr, n_in), lambda l, i: (l, i, 0))],
        out_specs=(pl.BlockSpec((1, tr, NZ), lambda l, i: (l, i, 0)),
                   pl.BlockSpec((1, tr, 128), lambda l, i: (l, i, 0))),
        compiler_params=pltpu.CompilerParams(
            dimension_semantics=("parallel", "parallel"),
            vmem_limit_bytes=VMEM_LIMIT),
        name="wprep",
    )(w_in)


def _inproj_kernel(x_ref, g_ref, w_ref, wg_ref, z_ref, zg_ref, xn_ref, *, n_sub):
    @pl.when(pl.program_id(1) == 0)
    def _():
        x = x_ref[...]
        ms = jnp.mean(x * x, axis=-1, keepdims=True)
        xn = (x * lax.rsqrt(ms + EPS) * g_ref[...]).astype(BF16)
        xn_ref[...] = xn
        zg_ref[...] = jnp.dot(xn, wg_ref[...], preferred_element_type=F32)

    xn = xn_ref[...]
    ts = w_ref.shape[1] // n_sub
    for n in range(n_sub):
        z_ref[:, n * ts:(n + 1) * ts] = jnp.dot(
            xn, w_ref[:, n * ts:(n + 1) * ts], preferred_element_type=F32).astype(z_ref.dtype)


def _inproj(x2, g, wz, wg):
    t = x2.shape[0]
    tm = _largest_divisor(t, (1024, 512, 256, 128))
    tn, n_sub = 3584, 7
    return pl.pallas_call(
        functools.partial(_inproj_kernel, n_sub=n_sub),
        out_shape=(jax.ShapeDtypeStruct((t, NZ), BF16),
                   jax.ShapeDtypeStruct((t, 128), F32)),
        grid=(t // tm, NZ // tn),
        in_specs=[pl.BlockSpec((tm, D_MODEL), lambda i, j: (i, 0)),
                  pl.BlockSpec((1, D_MODEL), lambda i, j: (0, 0)),
                  pl.BlockSpec((D_MODEL, tn), lambda i, j: (0, j)),
                  pl.BlockSpec((D_MODEL, 128), lambda i, j: (0, 0))],
        out_specs=(pl.BlockSpec((tm, tn), lambda i, j: (i, j)),
                   pl.BlockSpec((tm, 128), lambda i, j: (i, 0))),
        scratch_shapes=[pltpu.VMEM((tm, D_MODEL), BF16)],
        compiler_params=pltpu.CompilerParams(
            dimension_semantics=("parallel", "arbitrary"),
            vmem_limit_bytes=VMEM_LIMIT),
        name="inproj",
    )(x2, g, wz, wg)


def _gates_kernel(zg_ref, bias_ref, gcol_ref, grow_ref):
    row = lax.broadcasted_iota(jnp.int32, (CHUNK, 128), 0)
    lane = lax.broadcasted_iota(jnp.int32, (CHUNK, 128), 1)
    is_fox = lane < G_LI
    is_li = (lane >= G_LI) & (lane < G_LF)
    is_lf = (lane >= G_LF) & (lane < G_LF + ML_HEADS)
    tri = (row >= lane).astype(BF16)
    carry = jnp.zeros((1, 128), F32)
    for c in range(zg_ref.shape[1] // CHUNK):
        r0 = c * CHUNK
        a = zg_ref[0, r0:r0 + CHUNK, :] + bias_ref[...]
        log_sig = jnp.minimum(a, 0.0) - jnp.log1p(jnp.exp(-jnp.abs(a)))
        real = (row >= N_ZERO) if c == 0 else None
        v = jnp.where(is_fox | is_lf, log_sig, 0.0)
        if real is not None:
            v = jnp.where(real, v, 0.0)

        hi = v.astype(BF16)
        r1 = v - hi.astype(F32)
        mid = r1.astype(BF16)
        lo = (r1 - mid.astype(F32)).astype(BF16)
        cum = (jnp.dot(tri, hi, preferred_element_type=F32)
               + jnp.dot(tri, mid, preferred_element_type=F32)
               + jnp.dot(tri, lo, preferred_element_type=F32))
        cum = cum + carry
        carry = jnp.where(lane[:1] < G_LI, cum[CHUNK - 1:CHUNK, :], 0.0)

        log_i = a if real is None else jnp.where(real, a, NEG)
        out = jnp.where(is_li, log_i, cum)
        gcol_ref[0, r0:r0 + CHUNK, :] = out
        grow_ref[0, :, r0:r0 + CHUNK] = out.T[:16, :]


def _gates(zg3, bias):
    b, lp, _ = zg3.shape
    return pl.pallas_call(
        _gates_kernel,
        out_shape=(jax.ShapeDtypeStruct((b, lp, 128), F32),
                   jax.ShapeDtypeStruct((b, 16, lp), F32)),
        grid=(b,),
        in_specs=[pl.BlockSpec((1, lp, 128), lambda i: (i, 0, 0)),
                  pl.BlockSpec((1, 128), lambda i: (0, 0))],
        out_specs=(pl.BlockSpec((1, lp, 128), lambda i: (i, 0, 0)),
                   pl.BlockSpec((1, 16, lp), lambda i: (i, 0, 0))),
        compiler_params=pltpu.CompilerParams(dimension_semantics=("parallel",)),
        name="gates",
    )(zg3, bias)


def _fox_kernel(q_ref, k_ref, v_ref, fz_ref, gcol_ref, o_ref, kx_ref, vt_ref, *, tq, nq):
    h = pl.program_id(1)
    lp = k_ref.shape[1]
    r128 = lax.broadcasted_iota(jnp.int32, (128, 128), 0)
    c128 = lax.broadcasted_iota(jnp.int32, (128, 128), 1)
    sel = [((r128 == G_FOX + h) & (c128 == j)).astype(BF16) for j in range(3)]

    for c in range(lp // 128):
        r0 = c * 128
        kx_ref[r0:r0 + 128, 0:FOX_DH] = k_ref[0, r0:r0 + 128, :]
        g = gcol_ref[0, r0:r0 + 128, :] * (-LOG2E)
        hi = g.astype(BF16)
        r1 = g - hi.astype(F32)
        mid = r1.astype(BF16)
        lo = (r1 - mid.astype(F32)).astype(BF16)
        aug = (jnp.dot(hi, sel[0], preferred_element_type=F32)
               + jnp.dot(mid, sel[1], preferred_element_type=F32)
               + jnp.dot(lo, sel[2], preferred_element_type=F32))
        kx_ref[r0:r0 + 128, FOX_DH:2 * FOX_DH] = aug.astype(BF16)
        vt_ref[0:FOX_DH, r0:r0 + 128] = v_ref[0, r0:r0 + 128, :].astype(F32).T.astype(BF16)
    ones_row = lax.broadcasted_iota(jnp.int32, (16, lp), 0) == 0
    vt_ref[FOX_DH:FOX_DH + 16, :] = jnp.where(ones_row, 1.0, 0.0).astype(BF16)

    def extend_q(q):
        lane = lax.broadcasted_iota(jnp.int32, q.shape, 1)
        return jnp.concatenate([q, jnp.where(lane < 3, 1.0, 0.0).astype(BF16)], axis=1)

    def scores(qx, ks, tk, mask_fn):
        s = lax.dot_general(kx_ref[ks:ks + tk, :], qx, (((1,), (1,)), ((), ())),
                            preferred_element_type=F32)
        if mask_fn is not None:
            key = lax.broadcasted_iota(jnp.int32, s.shape, 0)
            qry = lax.broadcasted_iota(jnp.int32, s.shape, 1)
            s = jnp.where(mask_fn(key, qry), s, NEG)
        return s

    def partial_softmax(s, ks, tk):
        m = jnp.max(s, axis=0, keepdims=True)
        p = jnp.exp2(s - m).astype(BF16)
        acc = jnp.dot(vt_ref[:, ks:ks + tk], p, preferred_element_type=F32)
        return m, acc

    def finish(parts, q0, n):
        m = functools.reduce(jnp.maximum, [pm for pm, _ in parts])
        acc = 0.0
        for pm, pacc in parts:
            acc = acc + jnp.exp2(pm - m) * pacc
        fz = fz_ref[0, q0:q0 + n, :].astype(F32)
        o = (acc[0:FOX_DH] / acc[FOX_DH:FOX_DH + 1]).T * _silu(fz)
        o_ref[0, q0:q0 + n, :] = o.astype(o_ref.dtype)

    work = [(0, PAD, 0, PAD, lambda key, qry: (key >= N_ZERO) & (key <= qry))]
    for i in range(nq):
        q0 = PAD + i * tq
        work.append((q0, tq, 0, PAD, lambda key, qry: key >= N_ZERO))
        work += [(q0, tq, PAD + j * tq, tq, None) for j in range(i)]
        work.append((q0, tq, q0, tq, lambda key, qry: key <= qry))

    qx_of = {}

    def issue(item):
        q0, n, ks, tk, mask_fn = item
        if q0 not in qx_of:
            qx_of.clear()
            qx_of[q0] = extend_q(q_ref[0, q0:q0 + n, :])
        return scores(qx_of[q0], ks, tk, mask_fn)

    parts = []
    s_next = issue(work[0])
    for idx, (q0, n, ks, tk, _) in enumerate(work):
        s_cur = s_next
        if idx + 1 < len(work):
            s_next = issue(work[idx + 1])
        parts.append(partial_softmax(s_cur, ks, tk))
        if idx + 1 == len(work) or work[idx + 1][0] != q0:
            finish(parts, q0, n)
            parts = []


def _fox(z3, gcol):
    b, lp, _ = z3.shape
    n_real = lp - PAD
    tq = _largest_divisor(n_real, (512, 256, 128))
    blk = lambda off: pl.BlockSpec((1, lp, FOX_DH),
                                   lambda i, h, off=off: (i, 0, off // FOX_DH + h))
    return pl.pallas_call(
        functools.partial(_fox_kernel, tq=tq, nq=n_real // tq),
        out_shape=jax.ShapeDtypeStruct((b, lp, FOX_W), BF16),
        grid=(b, FOX_HEADS),
        in_specs=[blk(Z_FQ), blk(Z_FK), blk(Z_FV), blk(Z_FZ),
                  pl.BlockSpec((1, lp, 128), lambda i, h: (i, 0, 0))],
        out_specs=pl.BlockSpec((1, lp, FOX_DH), lambda i, h: (i, 0, h)),
        scratch_shapes=[pltpu.VMEM((lp, 2 * FOX_DH), BF16),
                        pltpu.VMEM((FOX_DH + 16, lp), BF16)],
        compiler_params=pltpu.CompilerParams(
            dimension_semantics=("parallel", "arbitrary"),
            vmem_limit_bytes=VMEM_LIMIT),
        name="fox",
    )(z3, z3, z3, z3, gcol)


def _mlstm_kernel(mq_ref, mk_ref, mv_ref, mo_ref, mz_ref, gcol_ref, grow_ref,
                  cw_ref, cb_ref, gh_ref, o_ref,
                  xprev_ref, c_ref, n_ref, m_ref):
    c = pl.program_id(1)

    @pl.when(c == 0)
    def _():
        xprev_ref[...] = jnp.zeros_like(xprev_ref)
        c_ref[...] = jnp.zeros_like(c_ref)
        n_ref[...] = jnp.zeros_like(n_ref)
        m_ref[...] = jnp.zeros_like(m_ref)

    half = ML_HEADS * ML_DK
    x_cur = jnp.concatenate([mq_ref[0], mk_ref[0]], axis=1)
    x_cat = jnp.concatenate([x_cur, xprev_ref[...]], axis=0)
    xprev_ref[...] = x_cur
    r = lax.broadcasted_iota(jnp.int32, ((CONV_W - 1) * CHUNK, 2 * CHUNK), 0)
    col = lax.broadcasted_iota(jnp.int32, ((CONV_W - 1) * CHUNK, 2 * CHUNK), 1)
    t = r % CHUNK
    src = t - (r // CHUNK + 1)
    shift = jnp.where(col == jnp.where(src < 0, src + 2 * CHUNK, src), 1.0, 0.0).astype(BF16)
    shifted = jnp.dot(shift, x_cat, preferred_element_type=F32)
    y = cb_ref[...] + cw_ref[CONV_W - 1:CONV_W, :] * x_cur.astype(F32)
    for j in range(1, CONV_W):
        y = y + cw_ref[CONV_W - 1 - j:CONV_W - j, :] * shifted[(j - 1) * CHUNK:j * CHUNK, :]
    qk = _silu(y)

    gcol = gcol_ref[0]
    grow = grow_ref[0]
    rows = lax.broadcasted_iota(jnp.int32, (CHUNK, CHUNK), 0)
    cols = lax.broadcasted_iota(jnp.int32, (CHUNK, CHUNK), 1)
    causal = cols <= rows

    for hd in range(ML_HEADS):
        q = (qk[:, hd * ML_DK:(hd + 1) * ML_DK] * (ML_DK ** -0.5)).astype(BF16)
        k = qk[:, half + hd * ML_DK: half + (hd + 1) * ML_DK]
        v = mv_ref[0, :, hd * ML_DV:(hd + 1) * ML_DV]
        b_col = gcol[:, G_LF + hd:G_LF + hd + 1]
        li_col = gcol[:, G_LI + hd:G_LI + hd + 1]
        b_row = grow[G_LF + hd:G_LF + hd + 1, :]
        li_row = grow[G_LI + hd:G_LI + hd + 1, :]
        m_prev = m_ref[hd:hd + 1, 0:1]
        c_prev = c_ref[hd]
        n_prev = n_ref[hd:hd + 1, :]

        log_d = jnp.where(causal, b_col - b_row + li_row, NEG)
        log_inter = b_col + m_prev
        m_t = jnp.maximum(log_inter, jnp.max(log_d, axis=-1, keepdims=True))
        d = jnp.exp(log_d - m_t)
        inter = jnp.exp(log_inter - m_t)
        s = lax.dot_general(q, k.astype(BF16), (((1,), (1,)), ((), ())),
                            preferred_element_type=F32) * d
        num = (inter * jnp.dot(q, c_prev.astype(BF16), preferred_element_type=F32)
               + jnp.dot(s.astype(BF16), v, preferred_element_type=F32))
        den = (inter * jnp.sum(q.astype(F32) * n_prev, axis=-1, keepdims=True)
               + jnp.sum(s, axis=-1, keepdims=True))
        hh = num / jnp.maximum(jnp.abs(den), jnp.exp(-m_t))

        b_last = b_col[CHUNK - 1:CHUNK, :]
        log_w = b_last - b_col + li_col
        m_new = jnp.maximum(b_last + m_prev, jnp.max(log_w, axis=0, keepdims=True))
        w = jnp.exp(log_w - m_new)
        decay = jnp.exp(b_last + m_prev - m_new)
        kw = k * w
        c_ref[hd] = decay * c_prev + jnp.dot(kw.T.astype(BF16), v, preferred_element_type=F32)
        n_ref[hd:hd + 1, :] = decay * n_prev + jnp.sum(kw, axis=0, keepdims=True)
        m_ref[hd:hd + 1, :] = jnp.broadcast_to(m_new, (1, 128))

        mo = mo_ref[0, :, hd * ML_DV:(hd + 1) * ML_DV].astype(F32)
        mz = mz_ref[0, :, hd * ML_DV:(hd + 1) * ML_DV].astype(F32)
        hb = _sigmoid(mo) * hh
        ms = jnp.mean(hb * hb, axis=-1, keepdims=True)
        hb = hb * lax.rsqrt(ms + EPS) * gh_ref[:, hd * ML_DV:(hd + 1) * ML_DV]
        o_ref[0, :, hd * ML_DV:(hd + 1) * ML_DV] = (hb * _silu(mz)).astype(o_ref.dtype)


def _mlstm(z3, gcol, grow, conv_w, conv_b, g_head):
    b, lp, _ = z3.shape
    half = ML_HEADS * ML_DK
    zblk = lambda width, off: pl.BlockSpec(
        (1, CHUNK, width), lambda i, c, off=off, width=width: (i, c, off // width))
    full = lambda shape: pl.BlockSpec(shape, lambda i, c: (0,) * len(shape))
    return pl.pallas_call(
        _mlstm_kernel,
        out_shape=jax.ShapeDtypeStruct((b, lp, ML_W), BF16),
        grid=(b, lp // CHUNK),
        in_specs=[zblk(half, Z_MQ), zblk(half, Z_MK), zblk(ML_W, Z_MV),
                  zblk(ML_W, Z_MO), zblk(ML_W, Z_MZ),
                  pl.BlockSpec((1, CHUNK, 128), lambda i, c: (i, c, 0)),
                  pl.BlockSpec((1, 16, CHUNK), lambda i, c: (i, 0, c)),
                  full((CONV_W, 2 * half)), full((1, 2 * half)), full((1, ML_W))],
        out_specs=pl.BlockSpec((1, CHUNK, ML_W), lambda i, c: (i, c, 0)),
        scratch_shapes=[pltpu.VMEM((CHUNK, 2 * half), BF16),
                        pltpu.VMEM((ML_HEADS, ML_DK, ML_DV), F32),
                        pltpu.VMEM((ML_HEADS, ML_DK), F32),
                        pltpu.VMEM((ML_HEADS, 128), F32)],
        compiler_params=pltpu.CompilerParams(
            dimension_semantics=("parallel", "arbitrary"),
            vmem_limit_bytes=VMEM_LIMIT),
        name="mlstm",
    )(z3, z3, z3, z3, z3, gcol, grow, conv_w, conv_b, g_head)


def _outproj_kernel(oa_ref, hb_ref, ga_ref, gb_ref, x_ref, wa_ref, wb_ref, wo_ref,
                    g_ref, o_ref):
    ya = jnp.dot(oa_ref[...], wa_ref[...], preferred_element_type=F32)
    yb = jnp.dot(hb_ref[...], wb_ref[...], preferred_element_type=F32)
    merged = (_sigmoid(ga_ref[...].astype(F32)) * ya
              + _sigmoid(gb_ref[...].astype(F32)) * yb)
    y = jnp.dot(merged.astype(BF16), wo_ref[...], preferred_element_type=F32)
    ms = jnp.mean(y * y, axis=-1, keepdims=True)
    o_ref[...] = x_ref[...] + y * lax.rsqrt(ms + EPS) * g_ref[...]


def _outproj(oa2, hb2, z2, x2, wa, wb, wo, g):
    t = x2.shape[0]
    tm = _largest_divisor(t, (512, 256, 128))
    rows = lambda width, col=0: pl.BlockSpec((tm, width), lambda i, col=col: (i, col))
    full = lambda shape: pl.BlockSpec(shape, lambda i: (0,) * len(shape))
    return pl.pallas_call(
        _outproj_kernel,
        out_shape=jax.ShapeDtypeStruct((t, D_MODEL), F32),
        grid=(t // tm,),
        in_specs=[rows(FOX_W), rows(ML_W),
                  rows(D_MODEL, Z_GA // D_MODEL), rows(D_MODEL, Z_GB // D_MODEL),
                  rows(D_MODEL),
                  full((FOX_W, D_MODEL)), full((ML_W, D_MODEL)), full((D_MODEL, D_MODEL)),
                  full((1, D_MODEL))],
        out_specs=rows(D_MODEL),
        compiler_params=pltpu.CompilerParams(
            dimension_semantics=("parallel",),
            vmem_limit_bytes=VMEM_LIMIT),
        name="outproj",
    )(oa2, hb2, z2, z2, x2, wa, wb, wo, g)


def _layer(h, g_pre, g_post, wz, wg, bias, conv_w, conv_b, g_head, w_a, w_b, w_o):
    b, lp, d = h.shape
    x2 = h.reshape(b * lp, d)
    z2, zg2 = _inproj(x2, g_pre[None, :], wz, wg)
    z3 = z2.reshape(b, lp, NZ)
    gcol, grow = _gates(zg2.reshape(b, lp, 128), bias[None, :])
    oa = _fox(z3, gcol)
    hb = _mlstm(z3, gcol, grow, conv_w, conv_b[None, :], g_head[None, :])
    out = _outproj(oa.reshape(b * lp, FOX_W), hb.reshape(b * lp, ML_W), z2, x2,
                   w_a, w_b, w_o, g_post[None, :])
    return out.reshape(b, lp, d)


def kernel(x, meta_tokens, norm_pre, norm_post, w_in, b_fox_f, conv_w, conv_b,
           b_mlstm_i, b_mlstm_f, mlstm_head_norm, w_a, w_b, w_o):
    b, seq, d = x.shape
    depth = norm_pre.shape[0]
    assert d == D_MODEL and seq % CHUNK == 0
    assert meta_tokens.shape == (N_META, D_MODEL)

    assert w_in.shape[1:] == (D_MODEL, N_IN)
    wz, wg = _wprep(w_in)
    bias = jnp.concatenate([b_fox_f, b_mlstm_i, b_mlstm_f, jnp.zeros((depth, 128 - 16), F32)], axis=1)
    w_a, w_b, w_o = w_a.astype(BF16), w_b.astype(BF16), w_o.astype(BF16)

    lead = jnp.concatenate([jnp.zeros((N_ZERO, d), x.dtype), meta_tokens.astype(x.dtype)], axis=0)
    h = jnp.concatenate([jnp.broadcast_to(lead[None], (b, PAD, d)), x], axis=1)
    for l in range(depth):
        h = _layer(h, norm_pre[l], norm_post[l], wz[l], wg[l], bias[l], conv_w[l], conv_b[l],
                   mlstm_head_norm[l], w_a[l], w_b[l], w_o[l])
    return h[:, PAD:]
```

```python
import functools

import jax
import jax.numpy as jnp
import numpy as np
from jax import lax
from jax.experimental import pallas as pl
from jax.experimental.pallas import tpu as pltpu

F32 = jnp.float32
BF16 = jnp.bfloat16

D_MODEL = 1024
N_META = 16
PAD = 128
N_ZERO = PAD - N_META
FOX_HEADS = 8
FOX_DH = 128
FOX_W = FOX_HEADS * FOX_DH
ML_HEADS = 4
ML_DK = 256
ML_DV = 512
ML_W = ML_HEADS * ML_DV
CHUNK = 128
CONV_W = 4
EPS = 1e-6
NEG = -1e30
LOG2E = 1.4426950408889634

SPLIT_SIZES = (FOX_W, FOX_W, FOX_W, FOX_HEADS, FOX_W, 2 * ML_HEADS * ML_DK, ML_W,
               ML_HEADS, ML_HEADS, ML_W, ML_W, D_MODEL, D_MODEL)
SPLIT_POINTS = tuple(int(p) for p in np.cumsum(SPLIT_SIZES)[:-1])
N_IN = sum(SPLIT_SIZES)

Z_FQ, Z_FK, Z_FV, Z_FZ = 0, 1024, 2048, 3072
Z_MQ, Z_MK, Z_MV, Z_MO, Z_MZ = 4096, 5120, 6144, 8192, 10240
Z_GA, Z_GB = 12288, 13312
NZ = 14336
G_FOX, G_LI, G_LF = 0, 8, 12

VMEM_LIMIT = 48 * 1024 * 1024


def _sigmoid(x):
    return 0.5 * jnp.tanh(0.5 * x) + 0.5


def _silu(x):
    h = 0.5 * x
    return h * jnp.tanh(h) + h


def _largest_divisor(n, candidates):
    for c in candidates:
        if n % c == 0:
            return c
    raise ValueError(f"no tile in {candidates} divides {n}")


def _wprep_kernel(w_ref, wz_ref, wg_ref):
    x = w_ref[0]
    p_ff, p_fz, p_mi, p_mo = SPLIT_POINTS[2], SPLIT_POINTS[3], SPLIT_POINTS[6], SPLIT_POINTS[8]
    wz_ref[0, :, Z_FQ:Z_FK] = (x[:, 0:FOX_W] * (LOG2E * FOX_DH ** -0.5)).astype(BF16)
    wz_ref[0, :, Z_FK:Z_FZ] = x[:, FOX_W:p_ff].astype(BF16)
    wz_ref[0, :, Z_FZ:Z_MO] = x[:, p_fz:p_mi].astype(BF16)
    wz_ref[0, :, Z_MO:NZ] = x[:, p_mo:N_IN].astype(BF16)
    gates = jnp.concatenate([x[:, p_ff:p_fz], x[:, p_mi:p_mo],
                             jnp.zeros((x.shape[0], 128 - 16), F32)], axis=1)
    wg_ref[0] = gates.astype(BF16)


def _wprep(w_in):
    depth, d, n_in = w_in.shape
    tr = 128
    return pl.pallas_call(
        _wprep_kernel,
        out_shape=(jax.ShapeDtypeStruct((depth, d, NZ), BF16),
                   jax.ShapeDtypeStruct((depth, d, 128), BF16)),
        grid=(depth, d // tr),
        in_specs=[pl.BlockSpec((1, tr, n_in), lambda l, i: (l, i, 0))],
        out_specs=(pl.BlockSpec((1, tr, NZ), lambda l, i: (l, i, 0)),
                   pl.BlockSpec((1, tr, 128), lambda l, i: (l, i, 0))),
        compiler_params=pltpu.CompilerParams(
            dimension_semantics=("parallel", "parallel"),
            vmem_limit_bytes=VMEM_LIMIT),
        name="wprep",
    )(w_in)


def _inproj_kernel(x_ref, g_ref, w_ref, wg_ref, z_ref, zg_ref, xn_ref, *, n_sub):
    @pl.when(pl.program_id(1) == 0)
    def _():
        x = x_ref[...]
        ms = jnp.mean(x * x, axis=-1, keepdims=True)
        xn = (x * lax.rsqrt(ms + EPS) * g_ref[...]).astype(BF16)
        xn_ref[...] = xn
        zg_ref[...] = jnp.dot(xn, wg_ref[...], preferred_element_type=F32)

    xn = xn_ref[...]
    ts = w_ref.shape[1] // n_sub
    for n in range(n_sub):
        z_ref[:, n * ts:(n + 1) * ts] = jnp.dot(
            xn, w_ref[:, n * ts:(n + 1) * ts], preferred_element_type=F32).astype(z_ref.dtype)


def _inproj(x2, g, wz, wg):
    t = x2.shape[0]
    tm = _largest_divisor(t, (1024, 512, 256, 128))
    tn, n_sub = 3584, 7
    return pl.pallas_call(
        functools.partial(_inproj_kernel, n_sub=n_sub),
        out_shape=(jax.ShapeDtypeStruct((t, NZ), BF16),
                   jax.ShapeDtypeStruct((t, 128), F32)),
        grid=(t // tm, NZ // tn),
        in_specs=[pl.BlockSpec((tm, D_MODEL), lambda i, j: (i, 0)),
                  pl.BlockSpec((1, D_MODEL), lambda i, j: (0, 0)),
                  pl.BlockSpec((D_MODEL, tn), lambda i, j: (0, j)),
                  pl.BlockSpec((D_MODEL, 128), lambda i, j: (0, 0))],
        out_specs=(pl.BlockSpec((tm, tn), lambda i, j: (i, j)),
                   pl.BlockSpec((tm, 128), lambda i, j: (i, 0))),
        scratch_shapes=[pltpu.VMEM((tm, D_MODEL), BF16)],
        compiler_params=pltpu.CompilerParams(
            dimension_semantics=("parallel", "arbitrary"),
            vmem_limit_bytes=VMEM_LIMIT),
        name="inproj",
    )(x2, g, wz, wg)


def _gates_kernel(zg_ref, bias_ref, gcol_ref, grow_ref):
    row = lax.broadcasted_iota(jnp.int32, (CHUNK, 128), 0)
    lane = lax.broadcasted_iota(jnp.int32, (CHUNK, 128), 1)
    is_fox = lane < G_LI
    is_li = (lane >= G_LI) & (lane < G_LF)
    is_lf = (lane >= G_LF) & (lane < G_LF + ML_HEADS)
    tri = (row >= lane).astype(BF16)
    carry = jnp.zeros((1, 128), F32)
    for c in range(zg_ref.shape[1] // CHUNK):
        r0 = c * CHUNK
        a = zg_ref[0, r0:r0 + CHUNK, :] + bias_ref[...]
        log_sig = jnp.minimum(a, 0.0) - jnp.log1p(jnp.exp(-jnp.abs(a)))
        real = (row >= N_ZERO) if c == 0 else None
        v = jnp.where(is_fox | is_lf, log_sig, 0.0)
        if real is not None:
            v = jnp.where(real, v, 0.0)

        hi = v.astype(BF16)
        r1 = v - hi.astype(F32)
        mid = r1.astype(BF16)
        lo = (r1 - mid.astype(F32)).astype(BF16)
        cum = (jnp.dot(tri, hi, preferred_element_type=F32)
               + jnp.dot(tri, mid, preferred_element_type=F32)
               + jnp.dot(tri, lo, preferred_element_type=F32))
        cum = cum + carry
        carry = jnp.where(lane[:1] < G_LI, cum[CHUNK - 1:CHUNK, :], 0.0)

        log_i = a if real is None else jnp.where(real, a, NEG)
        out = jnp.where(is_li, log_i, cum)
        gcol_ref[0, r0:r0 + CHUNK, :] = out
        grow_ref[0, :, r0:r0 + CHUNK] = out.T[:16, :]


def _gates(zg3, bias):
    b, lp, _ = zg3.shape
    return pl.pallas_call(
        _gates_kernel,
        out_shape=(jax.ShapeDtypeStruct((b, lp, 128), F32),
                   jax.ShapeDtypeStruct((b, 16, lp), F32)),
        grid=(b,),
        in_specs=[pl.BlockSpec((1, lp, 128), lambda i: (i, 0, 0)),
                  pl.BlockSpec((1, 128), lambda i: (0, 0))],
        out_specs=(pl.BlockSpec((1, lp, 128), lambda i: (i, 0, 0)),
                   pl.BlockSpec((1, 16, lp), lambda i: (i, 0, 0))),
        compiler_params=pltpu.CompilerParams(dimension_semantics=("parallel",)),
        name="gates",
    )(zg3, bias)


def _fox_kernel(q_ref, k_ref, v_ref, fz_ref, gcol_ref, o_ref, kx_ref, vt_ref, *, tq, nq):
    h = pl.program_id(1)
    lp = k_ref.shape[1]
    r128 = lax.broadcasted_iota(jnp.int32, (128, 128), 0)
    c128 = lax.broadcasted_iota(jnp.int32, (128, 128), 1)
    sel = [((r128 == G_FOX + h) & (c128 == j)).astype(BF16) for j in range(3)]

    for c in range(lp // 128):
        r0 = c * 128
        kx_ref[r0:r0 + 128, 0:FOX_DH] = k_ref[0, r0:r0 + 128, :]
        g = gcol_ref[0, r0:r0 + 128, :] * (-LOG2E)
        hi = g.astype(BF16)
        r1 = g - hi.astype(F32)
        mid = r1.astype(BF16)
        lo = (r1 - mid.astype(F32)).astype(BF16)
        aug = (jnp.dot(hi, sel[0], preferred_element_type=F32)
               + jnp.dot(mid, sel[1], preferred_element_type=F32)
               + jnp.dot(lo, sel[2], preferred_element_type=F32))
        kx_ref[r0:r0 + 128, FOX_DH:2 * FOX_DH] = aug.astype(BF16)
        vt_ref[0:FOX_DH, r0:r0 + 128] = v_ref[0, r0:r0 + 128, :].astype(F32).T.astype(BF16)
    ones_row = lax.broadcasted_iota(jnp.int32, (16, lp), 0) == 0
    vt_ref[FOX_DH:FOX_DH + 16, :] = jnp.where(ones_row, 1.0, 0.0).astype(BF16)

    def extend_q(q):
        lane = lax.broadcasted_iota(jnp.int32, q.shape, 1)
        return jnp.concatenate([q, jnp.where(lane < 3, 1.0, 0.0).astype(BF16)], axis=1)

    def scores(qx, ks, tk, mask_fn):
        s = lax.dot_general(kx_ref[ks:ks + tk, :], qx, (((1,), (1,)), ((), ())),
                            preferred_element_type=F32)
        if mask_fn is not None:
            key = lax.broadcasted_iota(jnp.int32, s.shape, 0)
            qry = lax.broadcasted_iota(jnp.int32, s.shape, 1)
            s = jnp.where(mask_fn(key, qry), s, NEG)
        return s

    def partial_softmax(s, ks, tk):
        m = jnp.max(s, axis=0, keepdims=True)
        p = jnp.exp2(s - m).astype(BF16)
        acc = jnp.dot(vt_ref[:, ks:ks + tk], p, preferred_element_type=F32)
        return m, acc

    def finish(parts, q0, n):
        m = functools.reduce(jnp.maximum, [pm for pm, _ in parts])
        acc = 0.0
        for pm, pacc in parts:
            acc = acc + jnp.exp2(pm - m) * pacc
        fz = fz_ref[0, q0:q0 + n, :].astype(F32)
        o = (acc[0:FOX_DH] / acc[FOX_DH:FOX_DH + 1]).T * _silu(fz)
        o_ref[0, q0:q0 + n, :] = o.astype(o_ref.dtype)

    work = [(0, PAD, 0, PAD, lambda key, qry: (key >= N_ZERO) & (key <= qry))]
    for i in range(nq):
        q0 = PAD + i * tq
        work.append((q0, tq, 0, PAD, lambda key, qry: key >= N_ZERO))
        work += [(q0, tq, PAD + j * tq, tq, None) for j in range(i)]
        work.append((q0, tq, q0, tq, lambda key, qry: key <= qry))

    qx_of = {}

    def issue(item):
        q0, n, ks, tk, mask_fn = item
        if q0 not in qx_of:
            qx_of.clear()
            qx_of[q0] = extend_q(q_ref[0, q0:q0 + n, :])
        return scores(qx_of[q0], ks, tk, mask_fn)

    parts = []
    s_next = issue(work[0])
    for idx, (q0, n, ks, tk, _) in enumerate(work):
        s_cur = s_next
        if idx + 1 < len(work):
            s_next = issue(work[idx + 1])
        parts.append(partial_softmax(s_cur, ks, tk))
        if idx + 1 == len(work) or work[idx + 1][0] != q0:
            finish(parts, q0, n)
            parts = []


def _fox(z3, gcol):
    b, lp, _ = z3.shape
    n_real = lp - PAD
    tq = _largest_divisor(n_real, (512, 256, 128))
    blk = lambda off: pl.BlockSpec((1, lp, FOX_DH),
                                   lambda i, h, off=off: (i, 0, off // FOX_DH + h))
    return pl.pallas_call(
        functools.partial(_fox_kernel, tq=tq, nq=n_real // tq),
        out_shape=jax.ShapeDtypeStruct((b, lp, FOX_W), BF16),
        grid=(b, FOX_HEADS),
        in_specs=[blk(Z_FQ), blk(Z_FK), blk(Z_FV), blk(Z_FZ),
                  pl.BlockSpec((1, lp, 128), lambda i, h: (i, 0, 0))],
        out_specs=pl.BlockSpec((1, lp, FOX_DH), lambda i, h: (i, 0, h)),
        scratch_shapes=[pltpu.VMEM((lp, 2 * FOX_DH), BF16),
                        pltpu.VMEM((FOX_DH + 16, lp), BF16)],
        compiler_params=pltpu.CompilerParams(
            dimension_semantics=("parallel", "arbitrary"),
            vmem_limit_bytes=VMEM_LIMIT),
        name="fox",
    )(z3, z3, z3, z3, gcol)


def _mlstm_kernel(mq_ref, mk_ref, mv_ref, mo_ref, mz_ref, gcol_ref, grow_ref,
                  cw_ref, cb_ref, gh_ref, o_ref,
                  xprev_ref, c_ref, n_ref, m_ref):
    c = pl.program_id(1)

    @pl.when(c == 0)
    def _():
        xprev_ref[...] = jnp.zeros_like(xprev_ref)
        c_ref[...] = jnp.zeros_like(c_ref)
        n_ref[...] = jnp.zeros_like(n_ref)
        m_ref[...] = jnp.zeros_like(m_ref)

    half = ML_HEADS * ML_DK
    rows = lax.broadcasted_iota(jnp.int32, (CHUNK, CHUNK), 0)
    cols = lax.broadcasted_iota(jnp.int32, (CHUNK, CHUNK), 1)
    causal = cols <= rows

    for sub in range(mq_ref.shape[1] // CHUNK):
        r0 = sub * CHUNK
        x_cur = jnp.concatenate([mq_ref[0, r0:r0 + CHUNK, :], mk_ref[0, r0:r0 + CHUNK, :]],
                                axis=1)
        x_cat = jnp.concatenate([x_cur, xprev_ref[...]], axis=0)
        xprev_ref[...] = x_cur
        r = lax.broadcasted_iota(jnp.int32, ((CONV_W - 1) * CHUNK, 2 * CHUNK), 0)
        col = lax.broadcasted_iota(jnp.int32, ((CONV_W - 1) * CHUNK, 2 * CHUNK), 1)
        t = r % CHUNK
        src = t - (r // CHUNK + 1)
        shift = jnp.where(col == jnp.where(src < 0, src + 2 * CHUNK, src), 1.0, 0.0).astype(BF16)
        shifted = jnp.dot(shift, x_cat, preferred_element_type=F32)
        y = cb_ref[...] + cw_ref[CONV_W - 1:CONV_W, :] * x_cur.astype(F32)
        for j in range(1, CONV_W):
            y = y + cw_ref[CONV_W - 1 - j:CONV_W - j, :] * shifted[(j - 1) * CHUNK:j * CHUNK, :]
        qk = _silu(y)

        gcol = gcol_ref[0, r0:r0 + CHUNK, :]
        grow = grow_ref[0, :, r0:r0 + CHUNK]

        for hd in range(ML_HEADS):
            q = (qk[:, hd * ML_DK:(hd + 1) * ML_DK] * (ML_DK ** -0.5)).astype(BF16)
            k = qk[:, half + hd * ML_DK: half + (hd + 1) * ML_DK]
            v = mv_ref[0, r0:r0 + CHUNK, hd * ML_DV:(hd + 1) * ML_DV]
            b_col = gcol[:, G_LF + hd:G_LF + hd + 1]
            li_col = gcol[:, G_LI + hd:G_LI + hd + 1]
            b_row = grow[G_LF + hd:G_LF + hd + 1, :]
            li_row = grow[G_LI + hd:G_LI + hd + 1, :]
            m_prev = m_ref[hd:hd + 1, 0:1]
            c_prev = c_ref[hd]
            n_prev = n_ref[hd:hd + 1, :]

            log_d = jnp.where(causal, b_col - b_row + li_row, NEG)
            log_inter = b_col + m_prev
            m_t = jnp.maximum(log_inter, jnp.max(log_d, axis=-1, keepdims=True))
            d = jnp.exp(log_d - m_t)
            inter = jnp.exp(log_inter - m_t)
            s = lax.dot_general(q, k.astype(BF16), (((1,), (1,)), ((), ())),
                                preferred_element_type=F32) * d
            num = (inter * jnp.dot(q, c_prev.astype(BF16), preferred_element_type=F32)
                   + jnp.dot(s.astype(BF16), v, preferred_element_type=F32))
            den = (inter * jnp.sum(q.astype(F32) * n_prev, axis=-1, keepdims=True)
                   + jnp.sum(s, axis=-1, keepdims=True))
            hh = num / jnp.maximum(jnp.abs(den), jnp.exp(-m_t))

            b_last = b_col[CHUNK - 1:CHUNK, :]
            log_w = b_last - b_col + li_col
            m_new = jnp.maximum(b_last + m_prev, jnp.max(log_w, axis=0, keepdims=True))
            w = jnp.exp(log_w - m_new)
            decay = jnp.exp(b_last + m_prev - m_new)
            kw = k * w
            c_ref[hd] = decay * c_prev + jnp.dot(kw.T.astype(BF16), v,
                                                 preferred_element_type=F32)
            n_ref[hd:hd + 1, :] = decay * n_prev + jnp.sum(kw, axis=0, keepdims=True)
            m_ref[hd:hd + 1, :] = jnp.broadcast_to(m_new, (1, 128))

            mo = mo_ref[0, r0:r0 + CHUNK, hd * ML_DV:(hd + 1) * ML_DV].astype(F32)
            mz = mz_ref[0, r0:r0 + CHUNK, hd * ML_DV:(hd + 1) * ML_DV].astype(F32)
            hb = _sigmoid(mo) * hh
            ms = jnp.mean(hb * hb, axis=-1, keepdims=True)
            hb = hb * lax.rsqrt(ms + EPS) * gh_ref[:, hd * ML_DV:(hd + 1) * ML_DV]
            o_ref[0, r0:r0 + CHUNK, hd * ML_DV:(hd + 1) * ML_DV] = (
                hb * _silu(mz)).astype(o_ref.dtype)


def _mlstm(z3, gcol, grow, conv_w, conv_b, g_head):
    b, lp, _ = z3.shape
    half = ML_HEADS * ML_DK
    rows = CHUNK * _largest_divisor(lp // CHUNK, (3, 2, 1))
    zblk = lambda width, off: pl.BlockSpec(
        (1, rows, width), lambda i, c, off=off, width=width: (i, c, off // width))
    full = lambda shape: pl.BlockSpec(shape, lambda i, c: (0,) * len(shape))
    return pl.pallas_call(
        _mlstm_kernel,
        out_shape=jax.ShapeDtypeStruct((b, lp, ML_W), BF16),
        grid=(b, lp // rows),
        in_specs=[zblk(half, Z_MQ), zblk(half, Z_MK), zblk(ML_W, Z_MV),
                  zblk(ML_W, Z_MO), zblk(ML_W, Z_MZ),
                  pl.BlockSpec((1, rows, 128), lambda i, c: (i, c, 0)),
                  pl.BlockSpec((1, 16, rows), lambda i, c: (i, 0, c)),
                  full((CONV_W, 2 * half)), full((1, 2 * half)), full((1, ML_W))],
        out_specs=pl.BlockSpec((1, rows, ML_W), lambda i, c: (i, c, 0)),
        scratch_shapes=[pltpu.VMEM((CHUNK, 2 * half), BF16),
                        pltpu.VMEM((ML_HEADS, ML_DK, ML_DV), F32),
                        pltpu.VMEM((ML_HEADS, ML_DK), F32),
                        pltpu.VMEM((ML_HEADS, 128), F32)],
        compiler_params=pltpu.CompilerParams(
            dimension_semantics=("parallel", "arbitrary"),
            vmem_limit_bytes=VMEM_LIMIT),
        name="mlstm",
    )(z3, z3, z3, z3, z3, gcol, grow, conv_w, conv_b, g_head)


def _outproj_kernel(oa_ref, hb_ref, ga_ref, gb_ref, x_ref, wa_ref, wb_ref, wo_ref,
                    g_ref, o_ref):
    ya = jnp.dot(oa_ref[...], wa_ref[...], preferred_element_type=F32)
    yb = jnp.dot(hb_ref[...], wb_ref[...], preferred_element_type=F32)
    merged = (_sigmoid(ga_ref[...].astype(F32)) * ya
              + _sigmoid(gb_ref[...].astype(F32)) * yb)
    y = jnp.dot(merged.astype(BF16), wo_ref[...], preferred_element_type=F32)
    ms = jnp.mean(y * y, axis=-1, keepdims=True)
    o_ref[...] = x_ref[...] + y * lax.rsqrt(ms + EPS) * g_ref[...]


def _outproj(oa2, hb2, z2, x2, wa, wb, wo, g):
    t = x2.shape[0]
    tm = _largest_divisor(t, (512, 256, 128))
    rows = lambda width, col=0: pl.BlockSpec((tm, width), lambda i, col=col: (i, col))
    full = lambda shape: pl.BlockSpec(shape, lambda i: (0,) * len(shape))
    return pl.pallas_call(
        _outproj_kernel,
        out_shape=jax.ShapeDtypeStruct((t, D_MODEL), F32),
        grid=(t // tm,),
        in_specs=[rows(FOX_W), rows(ML_W),
                  rows(D_MODEL, Z_GA // D_MODEL), rows(D_MODEL, Z_GB // D_MODEL),
                  rows(D_MODEL),
                  full((FOX_W, D_MODEL)), full((ML_W, D_MODEL)), full((D_MODEL, D_MODEL)),
                  full((1, D_MODEL))],
        out_specs=rows(D_MODEL),
        compiler_params=pltpu.CompilerParams(
            dimension_semantics=("parallel",),
            vmem_limit_bytes=VMEM_LIMIT),
        name="outproj",
    )(oa2, hb2, z2, z2, x2, wa, wb, wo, g)


def _layer(h, g_pre, g_post, wz, wg, bias, conv_w, conv_b, g_head, w_a, w_b, w_o):
    b, lp, d = h.shape
    x2 = h.reshape(b * lp, d)
    z2, zg2 = _inproj(x2, g_pre[None, :], wz, wg)
    z3 = z2.reshape(b, lp, NZ)
    gcol, grow = _gates(zg2.reshape(b, lp, 128), bias[None, :])
    oa = _fox(z3, gcol)
    hb = _mlstm(z3, gcol, grow, conv_w, conv_b[None, :], g_head[None, :])
    out = _outproj(oa.reshape(b * lp, FOX_W), hb.reshape(b * lp, ML_W), z2, x2,
                   w_a, w_b, w_o, g_post[None, :])
    return out.reshape(b, lp, d)


def kernel(x, meta_tokens, norm_pre, norm_post, w_in, b_fox_f, conv_w, conv_b,
           b_mlstm_i, b_mlstm_f, mlstm_head_norm, w_a, w_b, w_o):
    b, seq, d = x.shape
    depth = norm_pre.shape[0]
    assert d == D_MODEL and seq % CHUNK == 0
    assert meta_tokens.shape == (N_META, D_MODEL)

    assert w_in.shape[1:] == (D_MODEL, N_IN)
    wz, wg = _wprep(w_in)
    bias = jnp.concatenate([b_fox_f, b_mlstm_i, b_mlstm_f, jnp.zeros((depth, 128 - 16), F32)], axis=1)
    w_a, w_b, w_o = w_a.astype(BF16), w_b.astype(BF16), w_o.astype(BF16)

    lead = jnp.concatenate([jnp.zeros((N_ZERO, d), x.dtype), meta_tokens.astype(x.dtype)], axis=0)
    h = jnp.concatenate([jnp.broadcast_to(lead[None], (b, PAD, d)), x], axis=1)
    for l in range(depth):
        h = _layer(h, norm_pre[l], norm_post[l], wz[l], wg[l], bias[l], conv_w[l], conv_b[l],
                   mlstm_head_norm[l], w_a[l], w_b[l], w_o[l])
    return h[:, PAD:]
```
